```python
import math
import jax, jax.numpy as jnp
from jax import lax
import numpy as np


D_MODEL = 2048
BATCH = 8
SEQ = 2048
DEPTH = 4
DEC_BATCH = 8
DEC_SEQ = 64
PAST_LEN = 2048

CHUNK = 64
N_EVEN = (DEPTH + 1) // 2
N_ODD = DEPTH // 2
EPS = 1e-6
SSM_HEAD_DIM = 64
SSM_INNER = D_MODEL
SSM_HEADS = SSM_INNER // SSM_HEAD_DIM
SSM_GROUPS = 4
SSM_STATE = 128
SSM_CONV = 4
SSM_CONV_DIM = SSM_INNER + 2 * SSM_GROUPS * SSM_STATE
ATT_HEAD_DIM = 64
ATT_INNER = D_MODEL
ATT_HEADS = ATT_INNER // ATT_HEAD_DIM
ATT_KV_HEADS = 8
ATT_GROUP = ATT_HEADS // ATT_KV_HEADS
KV_DIM = ATT_KV_HEADS * ATT_HEAD_DIM
WINDOW = 128
CONV_WIDTH = 2 * D_MODEL
SHORT_CONV = 3
EVEN_PROJ = 2 * SSM_INNER + 2 * SSM_GROUPS * SSM_STATE + SSM_HEADS + 2 * ATT_INNER + 2 * KV_DIM
EVEN_MIX = SSM_INNER + ATT_INNER

kernel_name = "hybrid_ssd_swa_shortconv_stream_step"


def rms_norm(x, w):
    xf = x.astype(jnp.float32)
    y = xf * lax.rsqrt(jnp.mean(xf * xf, axis=-1, keepdims=True) + EPS)
    return (y * w.astype(jnp.float32)).astype(x.dtype)


def causal_conv(x, prev, w):
    width, length = w.shape[0], x.shape[1]
    xp = jnp.concatenate([prev.astype(x.dtype), x], axis=1)
    w = w.astype(x.dtype)
    out = sum(w[i] * xp[:, i:i + length] for i in range(width))
    return out, xp[:, length:]


def modulate(x, c, w_ada, b_ada, g_pre):
    mod = jax.nn.silu(c) @ w_ada + b_ada
    shift, scale, gate = jnp.split(mod[:, None, :], 3, axis=-1)
    return rms_norm(x, g_pre) * (1 + scale) + shift, gate


def ssd_scan(x, dt, a, bm, cm, h0, chunk):
    b, length, nh, p = x.shape
    g, n = bm.shape[2], bm.shape[3]
    r = nh // g
    nc = length // chunk
    x = x.reshape(b, nc, chunk, g, r, p)
    dt = dt.reshape(b, nc, chunk, g, r)
    bm = bm.reshape(b, nc, chunk, g, n)
    cm = cm.reshape(b, nc, chunk, g, n)
    a_cum = jnp.cumsum(dt * a.reshape(g, r), axis=2)
    idx = jnp.arange(chunk)
    causal = (idx[:, None] >= idx[None, :])[None, None, :, :, None, None]
    seg = a_cum[:, :, :, None] - a_cum[:, :, None, :]
    decay = jnp.exp(jnp.where(causal, seg, -jnp.inf))
    cb = jnp.einsum('bcign,bcjgn->bcijg', cm, bm)
    mix = cb[..., None] * decay * dt[:, :, None]
    y_diag = jnp.einsum('bcijgr,bcjgrp->bcigrp', mix, x)
    a_last = a_cum[:, :, -1]
    to_end = jnp.exp(a_last[:, :, None] - a_cum) * dt
    chunk_states = jnp.einsum('bclgn,bclgr,bclgrp->bcgrpn', bm, to_end, x)

    def step(h, inp):
        s, al = inp
        return jnp.exp(al)[..., None, None] * h + s, h

    h_final, h_start = lax.scan(step, h0.reshape(b, g, r, p, n),
                                (jnp.moveaxis(chunk_states, 1, 0), jnp.moveaxis(a_last, 1, 0)))
    h_start = jnp.moveaxis(h_start, 0, 1)
    y_off = jnp.einsum('bcign,bcigr,bcgrpn->bcigrp', cm, jnp.exp(a_cum), h_start)
    return (y_diag + y_off).reshape(b, length, nh, p), h_final.reshape(b, nh, p, n)


def band_attention(q, k, v, valid, sinks):
    s = jnp.einsum('bnqkrd,bnskd->bnkrqs', q, k).astype(jnp.float32) * (ATT_HEAD_DIM ** -0.5)
    s = jnp.where(valid[None, :, None, None, None, :], s, -jnp.inf)
    sink = sinks.astype(jnp.float32).reshape(1, 1, ATT_KV_HEADS, ATT_GROUP, 1, 1)
    m = jnp.maximum(jnp.max(s, axis=-1, keepdims=True), sink)
    pr = jnp.exp(s - m)
    pr = pr / (jnp.sum(pr, axis=-1, keepdims=True) + jnp.exp(sink - m))
    return jnp.einsum('bnkrqs,bnskd->bnqkrd', pr.astype(v.dtype), v)


def even_mixer(h, conv_prev, ssm_prev, k_prev, v_prev, w_in, conv_w, conv_b, dt_bias, a_log, d_skip,
               norm_w, sinks, w_out, first):
    b, length, _ = h.shape
    f32 = jnp.float32
    cuts = np.cumsum([SSM_INNER, SSM_CONV_DIM, SSM_HEADS, ATT_INNER, KV_DIM, KV_DIM]).tolist()
    z, xbc, dt_raw, q, k, v, g_att = jnp.split(h @ w_in, cuts, axis=-1)
    xbc, conv_new = causal_conv(xbc, conv_prev, conv_w)
    xbc = jax.nn.silu(xbc + conv_b).astype(f32)
    xs, bm, cm = jnp.split(xbc, [SSM_INNER, SSM_INNER + SSM_GROUPS * SSM_STATE], axis=-1)
    xs = xs.reshape(b, length, SSM_HEADS, SSM_HEAD_DIM)
    dt = jax.nn.softplus(dt_raw.astype(f32) + dt_bias.astype(f32))
    a = -jnp.exp(a_log.astype(f32))
    y, ssm_new = ssd_scan(xs, dt, a, bm.reshape(b, length, SSM_GROUPS, SSM_STATE),
                          cm.reshape(b, length, SSM_GROUPS, SSM_STATE), ssm_prev.astype(f32),
                          CHUNK if first else length)
    y = (y + d_skip.astype(f32)[:, None] * xs).reshape(b, length, SSM_GROUPS, SSM_INNER // SSM_GROUPS)
    y = y * jax.nn.silu(z.astype(f32)).reshape(b, length, SSM_GROUPS, SSM_INNER // SSM_GROUPS)
    y = y * lax.rsqrt(jnp.mean(y * y, axis=-1, keepdims=True) + EPS)
    y_ssm = (y.reshape(b, length, SSM_INNER) * norm_w.astype(f32)).astype(h.dtype)
    q = q.reshape(b, length, ATT_KV_HEADS, ATT_GROUP, ATT_HEAD_DIM)
    kp = jnp.concatenate([k_prev.astype(h.dtype), k.reshape(b, length, ATT_KV_HEADS, ATT_HEAD_DIM)], axis=1)
    vp = jnp.concatenate([v_prev.astype(h.dtype), v.reshape(b, length, ATT_KV_HEADS, ATT_HEAD_DIM)], axis=1)
    if first:
        n = length // CHUNK
        back = WINDOW // CHUNK
        kc = kp.reshape(b, n + back, CHUNK, ATT_KV_HEADS, ATT_HEAD_DIM)
        vc = vp.reshape(b, n + back, CHUNK, ATT_KV_HEADS, ATT_HEAD_DIM)
        kb = jnp.concatenate([kc[:, j:j + n] for j in range(back + 1)], axis=2)
        vb = jnp.concatenate([vc[:, j:j + n] for j in range(back + 1)], axis=2)
        key_chunk = jnp.arange(n)[:, None] - back + jnp.arange(WINDOW + CHUNK)[None, :] // CHUNK
        valid = key_chunk >= 0
        qb = q.reshape(b, n, CHUNK, ATT_KV_HEADS, ATT_GROUP, ATT_HEAD_DIM)
    else:
        kb, vb = kp[:, None], vp[:, None]
        valid = jnp.ones((1, WINDOW + length), dtype=bool)
        qb = q[:, None]
    o = band_attention(qb, kb, vb, valid, sinks)
    y_att = o.reshape(b, length, ATT_INNER) * jax.nn.silu(g_att)
    out = jnp.concatenate([y_ssm, y_att], axis=-1) @ w_out
    return out, conv_new, ssm_new, kp[:, -WINDOW:], vp[:, -WINDOW:]


def odd_mixer(h, conv_prev, w_in, conv_w, w_out):
    u, gb, gc, g = jnp.split(h @ w_in, 4, axis=-1)
    conv_out, conv_new = causal_conv(gc * u, conv_prev, conv_w)
    y = gb * conv_out * jax.nn.silu(g)
    return y @ w_out, conv_new


def setup_inputs(seed: int = 0) -> dict:
    key = jax.random.key(seed)
    ks = jax.random.split(key, 32)

    def nrm(k, shape, s=1.0):
        return s * jax.random.normal(k, shape, jnp.float32)

    D = D_MODEL
    dt0 = jnp.exp(jax.random.uniform(ks[13], (N_EVEN, SSM_HEADS), jnp.float32, math.log(1e-3), math.log(1e-1)))
    return {
        'x_prompt': nrm(ks[0], (BATCH, SEQ, D)),
        'x_sample': nrm(ks[1], (DEC_BATCH, DEC_SEQ, D)),
        'c_prompt': nrm(ks[2], (BATCH, D)),
        'c_sample': nrm(ks[3], (DEC_BATCH, D)),
        'cache_k': nrm(ks[4], (N_EVEN, DEC_BATCH, WINDOW, ATT_KV_HEADS, ATT_HEAD_DIM)),
        'cache_v': nrm(ks[5], (N_EVEN, DEC_BATCH, WINDOW, ATT_KV_HEADS, ATT_HEAD_DIM)),
        'state_conv_a': nrm(ks[6], (N_EVEN, DEC_BATCH, SSM_CONV - 1, SSM_CONV_DIM)),
        'state_ssm': nrm(ks[7], (N_EVEN, DEC_BATCH, SSM_HEADS, SSM_HEAD_DIM, SSM_STATE), 0.1),
        'state_conv_c': nrm(ks[8], (N_ODD, DEC_BATCH, SHORT_CONV - 1, CONV_WIDTH)),
        'w_ada': nrm(ks[9], (DEPTH, D, 3 * D), D ** -0.5),
        'b_ada': nrm(ks[10], (DEPTH, 3 * D), 0.01),
        'norm_pre': 1.0 + nrm(ks[11], (DEPTH, D), 0.1),
        'norm_post': 1.0 + nrm(ks[12], (DEPTH, D), 0.1),
        'w_in_even': nrm(ks[14], (N_EVEN, D, EVEN_PROJ), D ** -0.5),
        'conv_a_w': nrm(ks[15], (N_EVEN, SSM_CONV, SSM_CONV_DIM), SSM_CONV ** -0.5),
        'conv_a_b': nrm(ks[16], (N_EVEN, SSM_CONV_DIM), 0.1),
        'dt_bias': dt0 + jnp.log(-jnp.expm1(-dt0)),
        'a_log': jnp.log(jax.random.uniform(ks[17], (N_EVEN, SSM_HEADS), jnp.float32, 1.0, 16.0)),
        'd_skip': 1.0 + nrm(ks[18], (N_EVEN, SSM_HEADS), 0.1),
        'norm_ssm': 1.0 + nrm(ks[19], (N_EVEN, SSM_INNER), 0.1),
        'sinks': nrm(ks[20], (N_EVEN, ATT_HEADS)),
        'w_out_even': nrm(ks[21], (N_EVEN, EVEN_MIX, D), EVEN_MIX ** -0.5),
        'w_in_odd': nrm(ks[22], (N_ODD, D, 4 * CONV_WIDTH), D ** -0.5),
        'conv_c_w': nrm(ks[23], (N_ODD, SHORT_CONV, CONV_WIDTH), SHORT_CONV ** -0.5),
        'w_out_odd': nrm(ks[24], (N_ODD, CONV_WIDTH, D), CONV_WIDTH ** -0.5),
    }


def reference(x_prompt, x_sample, c_prompt, c_sample, cache_k, cache_v, state_conv_a, state_ssm, state_conv_c,
              w_ada, b_ada, norm_pre, norm_post, w_in_even, conv_a_w, conv_a_b, dt_bias, a_log, d_skip, norm_ssm,
              sinks, w_out_even, w_in_odd, conv_c_w, w_out_odd):
    xp, xs = x_prompt, x_sample
    bp = xp.shape[0]
    kp_l, vp_l, cap_l, ssp_l, ccp_l = [], [], [], [], []
    ks_l, vs_l, cas_l, sss_l, ccs_l = [], [], [], [], []
    for layer in range(DEPTH):
        i = layer // 2
        hp, gate_p = modulate(xp, c_prompt, w_ada[layer], b_ada[layer], norm_pre[layer])
        hs, gate_s = modulate(xs, c_sample, w_ada[layer], b_ada[layer], norm_pre[layer])
        if layer % 2 == 0:
            ew = (w_in_even[i], conv_a_w[i], conv_a_b[i], dt_bias[i], a_log[i], d_skip[i], norm_ssm[i], sinks[i],
                  w_out_even[i])
            zkv = jnp.zeros((bp, WINDOW, ATT_KV_HEADS, ATT_HEAD_DIM), hp.dtype)
            op, conv_p, ssm_p, k_p, v_p = even_mixer(
                hp, jnp.zeros((bp, SSM_CONV - 1, SSM_CONV_DIM), hp.dtype),
                jnp.zeros((bp, SSM_HEADS, SSM_HEAD_DIM, SSM_STATE), jnp.float32), zkv, zkv, *ew, first=True)
            os_, conv_s, ssm_s, k_s, v_s = even_mixer(
                hs, state_conv_a[i], state_ssm[i], cache_k[i], cache_v[i], *ew, first=False)
            kp_l.append(k_p); vp_l.append(v_p); cap_l.append(conv_p); ssp_l.append(ssm_p)
            ks_l.append(k_s); vs_l.append(v_s); cas_l.append(conv_s); sss_l.append(ssm_s)
        else:
            op, cc_p = odd_mixer(hp, jnp.zeros((bp, SHORT_CONV - 1, CONV_WIDTH), hp.dtype),
                                 w_in_odd[i], conv_c_w[i], w_out_odd[i])
            os_, cc_s = odd_mixer(hs, state_conv_c[i], w_in_odd[i], conv_c_w[i], w_out_odd[i])
            ccp_l.append(cc_p); ccs_l.append(cc_s)
        xp = xp + gate_p * rms_norm(op, norm_post[layer])
        xs = xs + gate_s * rms_norm(os_, norm_post[layer])
    return (xp, xs,
            jnp.stack(kp_l), jnp.stack(vp_l), jnp.stack(cap_l), jnp.stack(ssp_l), jnp.stack(ccp_l),
            jnp.stack(ks_l), jnp.stack(vs_l), jnp.stack(cas_l), jnp.stack(sss_l), jnp.stack(ccs_l))
```

```python
import functools

import jax
import jax.numpy as jnp
from jax import lax
from jax.experimental import pallas as pl
from jax.experimental.pallas import tpu as pltpu

F32 = jnp.float32
BF16 = jnp.bfloat16

CHUNK = 64
WINDOW = 128
EPS = 1e-6
SSM_HEAD_DIM = 64
SSM_GROUPS = 4
SSM_STATE = 128
SSM_CONV = 4
ATT_HEAD_DIM = 64
ATT_KV_HEADS = 8
SHORT_CONV = 3
LANES = 128

VMEM_LIMIT = 56 * 1024 * 1024
TM_IN = 1024
TN_IN = 1024
TM_OUT = 512
TC_ODD = 256


def _cparams(sem):
    return pltpu.CompilerParams(dimension_semantics=sem, vmem_limit_bytes=VMEM_LIMIT)


def _dot(a, b):
    return jnp.dot(a, b, preferred_element_type=F32)


def _dot_nt(a, b):
    return lax.dot_general(a, b, (((1,), (1,)), ((), ())), preferred_element_type=F32)


def _dot_tn(a, b):
    return lax.dot_general(a, b, (((0,), (0,)), ((), ())), preferred_element_type=F32)


def _silu(x):
    return x * jax.nn.sigmoid(x)


def _mod_kernel(c_ref, w_ref, b_ref, o_ref):
    sc = _silu(c_ref[...]).astype(BF16)
    o_ref[0] = _dot(sc, w_ref[0].astype(BF16)) + b_ref[0]


def _mod_call(c_all, w_ada, b_ada):
    depth, d, d3 = w_ada.shape
    nb = c_all.shape[0]
    tn = 1024
    return pl.pallas_call(
        _mod_kernel,
        grid=(depth, d3 // tn),
        in_specs=[
            pl.BlockSpec((nb, d), lambda l, j: (0, 0)),
            pl.BlockSpec((1, d, tn), lambda l, j: (l, 0, j)),
            pl.BlockSpec((1, 1, tn), lambda l, j: (l, 0, j)),
        ],
        out_specs=pl.BlockSpec((1, nb, tn), lambda l, j: (l, 0, j)),
        out_shape=jax.ShapeDtypeStruct((depth, nb, d3), F32),
        compiler_params=_cparams(("arbitrary", "arbitrary")),
        name="adaln_mod",
    )(c_all, w_ada, b_ada.reshape(depth, 1, d3))


def _modulated_norm_into(x_ref, scale_ref, shift_ref, g_ref, h_scr, tm):
    g = g_ref[...]
    for cc in range(tm // CHUNK):
        rows = slice(cc * CHUNK, (cc + 1) * CHUNK)
        x = x_ref[rows, :]
        ms = jnp.mean(x * x, axis=-1, keepdims=True)
        y = x * lax.rsqrt(ms + EPS) * g
        h = y * (1.0 + scale_ref[cc:cc + 1, :]) + shift_ref[cc:cc + 1, :]
        h_scr[rows, :] = h.astype(BF16)


def _inproj_even_kernel(tm, x_ref, scale_ref, shift_ref, g_ref, w_ref, wdt_ref, o_ref, odt_ref, h_scr):
    @pl.when(pl.program_id(1) == 0)
    def _():
        _modulated_norm_into(x_ref, scale_ref, shift_ref, g_ref, h_scr, tm)
        odt_ref[...] = _dot(h_scr[...], wdt_ref[...])

    o_ref[...] = _dot(h_scr[...], w_ref[...])


def _inproj_even_call(x, scale_pc, shift_pc, g, w_main, w_dt, tm, tn):
    r, d = x.shape
    n = w_main.shape[1]
    nch = tm // CHUNK
    return pl.pallas_call(
        functools.partial(_inproj_even_kernel, tm),
        grid=(r // tm, n // tn),
        in_specs=[
            pl.BlockSpec((tm, d), lambda i, j: (i, 0)),
            pl.BlockSpec((nch, d), lambda i, j: (i, 0)),
            pl.BlockSpec((nch, d), lambda i, j: (i, 0)),
            pl.BlockSpec((1, d), lambda i, j: (0, 0)),
            pl.BlockSpec((d, tn), lambda i, j: (0, j)),
            pl.BlockSpec((d, LANES), lambda i, j: (0, 0)),
        ],
        out_specs=[
            pl.BlockSpec((tm, tn), lambda i, j: (i, j)),
            pl.BlockSpec((tm, LANES), lambda i, j: (i, 0)),
        ],
        out_shape=[
            jax.ShapeDtypeStruct((r, n), F32),
            jax.ShapeDtypeStruct((r, LANES), F32),
        ],
        scratch_shapes=[pltpu.VMEM((tm, d), BF16)],
        compiler_params=_cparams(("arbitrary", "arbitrary")),
        name="inproj_even",
    )(x, scale_pc, shift_pc, g, w_main, w_dt)


def _inproj_odd_kernel(tm, tc, tiles_per_seq, per_chunk_start,
                       x_ref, scale_ref, shift_ref, g_ref, w_ref, cw_ref, c0_ref,
                       y_ref, cn_ref, h_scr, ebuf, halo):
    i = pl.program_id(0)
    j = pl.program_id(1)

    @pl.when(j == 0)
    def _():
        _modulated_norm_into(x_ref, scale_ref, shift_ref, g_ref, h_scr, tm)

    p = _dot(h_scr[...], w_ref[...])
    w0 = cw_ref[0:1, :]
    w1 = cw_ref[1:2, :]
    w2 = cw_ref[2:3, :]
    nch = tm // CHUNK
    for cc in range(nch):
        rows = slice(cc * CHUNK, (cc + 1) * CHUNK)
        u = p[rows, 0:tc]
        gb = p[rows, tc:2 * tc]
        gc = p[rows, 2 * tc:3 * tc]
        g = p[rows, 3 * tc:4 * tc]
        if per_chunk_start:
            ebuf[6:8, :] = c0_ref[cc]
        elif cc == 0:
            @pl.when(i % tiles_per_seq == 0)
            def _():
                ebuf[6:8, :] = c0_ref[0]

            @pl.when(i % tiles_per_seq != 0)
            def _():
                ebuf[6:8, :] = halo[j, 6:8, :]
        else:
            ebuf[6:8, :] = ebuf[CHUNK + 6:CHUNK + 8, :]
        ebuf[8:8 + CHUNK, :] = gc * u
        conv = w0 * ebuf[6:6 + CHUNK, :] + w1 * ebuf[7:7 + CHUNK, :] + w2 * ebuf[8:8 + CHUNK, :]
        y_ref[rows, :] = (gb * conv * _silu(g)).astype(BF16)
        if per_chunk_start:
            cn_ref[cc] = ebuf[CHUNK + 6:CHUNK + 8, :]
    if not per_chunk_start:
        tail = ebuf[CHUNK + 6:CHUNK + 8, :]
        halo[j, 6:8, :] = tail
        cn_ref[0] = tail


def _inproj_odd_call(x, scale_pc, shift_pc, g, w_r, conv_w, conv0, seq_len, tm, tc):
    r, d = x.shape
    width = conv_w.shape[1]
    nct = width // tc
    nch = tm // CHUNK
    per_chunk_start = seq_len == CHUNK
    if per_chunk_start:
        tiles_per_seq = 1
        nb = nch
        c0_map = lambda i, j: (i, 0, j)
    else:
        assert seq_len % tm == 0
        tiles_per_seq = seq_len // tm
        nb = 1
        c0_map = lambda i, j: (i // tiles_per_seq, 0, j)
    y, tails = pl.pallas_call(
        functools.partial(_inproj_odd_kernel, tm, tc, tiles_per_seq, per_chunk_start),
        grid=(r // tm, nct),
        in_specs=[
            pl.BlockSpec((tm, d), lambda i, j: (i, 0)),
            pl.BlockSpec((nch, d), lambda i, j: (i, 0)),
            pl.BlockSpec((nch, d), lambda i, j: (i, 0)),
            pl.BlockSpec((1, d), lambda i, j: (0, 0)),
            pl.BlockSpec((d, 4 * tc), lambda i, j: (0, j)),
            pl.BlockSpec((SHORT_CONV, tc), lambda i, j: (0, j)),
            pl.BlockSpec((nb, SHORT_CONV - 1, tc), c0_map),
        ],
        out_specs=[
            pl.BlockSpec((tm, tc), lambda i, j: (i, j)),
            pl.BlockSpec((nb, SHORT_CONV - 1, tc), lambda i, j: (i, 0, j)),
        ],
        out_shape=[
            jax.ShapeDtypeStruct((r, width), BF16),
            jax.ShapeDtypeStruct((r // tm * nb, SHORT_CONV - 1, width), F32),
        ],
        scratch_shapes=[
            pltpu.VMEM((tm, d), BF16),
            pltpu.VMEM((CHUNK + 8, tc), F32),
            pltpu.VMEM((nct, 8, tc), F32),
        ],
        compiler_params=_cparams(("arbitrary", "arbitrary")),
        name="inproj_odd",
    )(x, scale_pc, shift_pc, g, w_r, conv_w, conv0)
    return y, tails[tiles_per_seq - 1::tiles_per_seq]


def _outproj_kernel(tm, y_ref, w_ref, x_ref, gate_ref, np_ref, o_ref):
    o = _dot(y_ref[...], w_ref[...])
    npw = np_ref[...]
    for cc in range(tm // CHUNK):
        rows = slice(cc * CHUNK, (cc + 1) * CHUNK)
        oc = o[rows, :]
        ms = jnp.mean(oc * oc, axis=-1, keepdims=True)
        r = oc * lax.rsqrt(ms + EPS) * npw
        o_ref[rows, :] = x_ref[rows, :] + gate_ref[cc:cc + 1, :] * r


def _outproj_call(y, w, x, gate_pc, npost, tm):
    r, k = y.shape
    d = w.shape[1]
    nch = tm // CHUNK
    return pl.pallas_call(
        functools.partial(_outproj_kernel, tm),
        grid=(r // tm,),
        in_specs=[
            pl.BlockSpec((tm, k), lambda i: (i, 0)),
            pl.BlockSpec((k, d), lambda i: (0, 0), pipeline_mode=pl.Buffered(1)),
            pl.BlockSpec((tm, d), lambda i: (i, 0)),
            pl.BlockSpec((nch, d), lambda i: (i, 0)),
            pl.BlockSpec((1, d), lambda i: (0, 0)),
        ],
        out_specs=pl.BlockSpec((tm, d), lambda i: (i, 0)),
        out_shape=jax.ShapeDtypeStruct((r, d), F32),
        compiler_params=_cparams(("arbitrary",)),
        name="outproj",
    )(y, w, x, gate_pc, npost)


def _write_kv_slot(kv, buf, base):
    lane = lax.broadcasted_iota(jnp.int32, (CHUNK, LANES), 1)
    lo = lane < ATT_HEAD_DIM
    for m in range(kv.shape[1] // LANES):
        blk = kv[:, m * LANES:(m + 1) * LANES]
        sw = pltpu.roll(blk, ATT_HEAD_DIM, 1)
        pieces = (
            (2 * m, jnp.where(lo, blk, 0.0), jnp.where(lo, 0.0, sw)),
            (2 * m + 1, jnp.where(lo, sw, 0.0), jnp.where(lo, 0.0, blk)),
        )
        for head, low, high in pieces:
            buf[head, pl.ds(base, CHUNK), 0:LANES] = low.astype(BF16)
            buf[head, pl.ds(base + CHUNK, CHUNK), 0:LANES] = high.astype(BF16)


def _even_mixer_kernel(first, dims,
                       proj_ref, pdt_ref, conv0_ref, ssm0_ref, k0_ref, v0_ref,
                       convw_ref, convb_ref, dtb_ref, alog_ref, dskip_ref, normw_ref, sink_ref, e_ref,
                       y_ref, convn_ref, ssmn_ref, kn_ref, vn_ref,
                       xext, xs_s, bc_s, ht_s, kbd, vbd):
    inner, gn, kvd = dims
    c = pl.program_id(1)
    nc = pl.num_programs(1)
    n_heads = inner // SSM_HEAD_DIM
    gw = inner // SSM_GROUPS
    off_xbc = inner
    off_q = off_xbc + inner + 2 * gn
    off_k = off_q + inner
    off_v = off_k + kvd
    off_g = off_v + kvd
    n_tr = inner // LANES

    lane64 = lax.broadcasted_iota(jnp.int32, (CHUNK, LANES), 1) < ATT_HEAD_DIM
    lane128 = lax.broadcasted_iota(jnp.int32, (2 * CHUNK, LANES), 1) < ATT_HEAD_DIM

    @pl.when(c == 0)
    def _init():
        xext[5:8, :] = conv0_ref[0]
        for t in range(n_tr):
            ht_s[:, t * LANES:(t + 1) * LANES] = ssm0_ref[0, 2 * t:2 * t + 2].reshape(LANES, SSM_STATE).T
        for s in range(WINDOW // CHUNK):
            _write_kv_slot(k0_ref[0, s * CHUNK:(s + 1) * CHUNK, :], kbd, s * LANES)
            _write_kv_slot(v0_ref[0, s * CHUNK:(s + 1) * CHUNK, :], vbd, s * LANES)
        rowi = lax.broadcasted_iota(jnp.int32, (3 * LANES, LANES), 0)
        lanei = lax.broadcasted_iota(jnp.int32, (3 * LANES, LANES), 1)
        ones_bd = jnp.where((rowi % LANES < CHUNK) == (lanei < ATT_HEAD_DIM), 1.0, 0.0).astype(BF16)
        for h in range(ATT_KV_HEADS):
            vbd[h, :, LANES:2 * LANES] = ones_bd

    xext[8:8 + CHUNK, :] = proj_ref[:, off_xbc:off_q]
    piece = 512
    for t in range((inner + 2 * gn) // piece):
        cs = slice(t * piece, (t + 1) * piece)
        acc = convb_ref[:, cs] + convw_ref[3:4, cs] * xext[8:8 + CHUNK, cs]
        for i in range(SSM_CONV - 1):
            acc = acc + convw_ref[i:i + 1, cs] * xext[5 + i:5 + i + CHUNK, cs]
        act = _silu(acc)
        if t * piece < inner:
            xs_s[:, cs] = act
        else:
            bc_s[:, t * piece - inner:(t + 1) * piece - inner] = act
    tail = xext[CHUNK + 5:CHUNK + 8, :]
    convn_ref[0] = tail
    xext[5:8, :] = tail

    dt = jax.nn.softplus(pdt_ref[...] + dtb_ref[...])
    da = dt * (-jnp.exp(alog_ref[...]))
    ri = lax.broadcasted_iota(jnp.int32, (CHUNK, CHUNK), 0)
    ci = lax.broadcasted_iota(jnp.int32, (CHUNK, CHUNK), 1)
    causal = ri >= ci
    tri = jnp.where(causal, 1.0, 0.0).astype(BF16)
    d_hi = da.astype(BF16)
    r1 = da - d_hi.astype(F32)
    d_mid = r1.astype(BF16)
    d_lo = (r1 - d_mid.astype(F32)).astype(BF16)
    a_cum = _dot(tri, d_hi) + _dot(tri, d_mid) + _dot(tri, d_lo)
    a_last = a_cum[CHUNK - 1:CHUNK, :]
    stack = jnp.concatenate([jnp.exp(a_cum), dt, jnp.exp(a_last - a_cum)], axis=0)
    s_hi = stack.astype(BF16)
    s_lo = (stack - s_hi.astype(F32)).astype(BF16)
    ex = _dot(s_hi, e_ref[...]) + _dot(s_lo, e_ref[...])
    ea = ex[0:CHUNK]
    xs = xs_s[...]
    xdt = xs * ex[CHUNK:2 * CHUNK]
    xdt_b = xdt.astype(BF16)
    xsc_b = (xdt * ex[2 * CHUNK:3 * CHUNK]).astype(BF16)
    act_t = jnp.concatenate([a_cum, jnp.zeros((CHUNK, LANES), F32)], axis=0).T

    pairs_per_group = gw // LANES
    for g in range(SSM_GROUPS):
        gcs = slice(g * gw, (g + 1) * gw)
        bg = bc_s[:, g * SSM_STATE:(g + 1) * SSM_STATE].astype(BF16)
        cg = bc_s[:, gn + g * SSM_STATE:gn + (g + 1) * SSM_STATE].astype(BF16)
        cb = _dot_nt(cg, bg)
        h_prev = ht_s[:, gcs]
        y_off = _dot(cg, h_prev.astype(BF16)) * ea[:, gcs]
        ht_s[:, gcs] = h_prev * ea[CHUNK - 1:CHUNK, gcs] + _dot_tn(bg, xsc_b[:, gcs])
        y_pairs = []
        for pm in range(pairs_per_group):
            m = g * pairs_per_group + pm
            ps = slice(m * LANES, (m + 1) * LANES)
            halves = []
            for e in range(2):
                h = 2 * m + e
                seg = a_cum[:, h:h + 1] - act_t[h:h + 1, 0:CHUNK]
                dec = jnp.exp(jnp.where(causal, seg, -jnp.inf))
                halves.append(_dot((cb * dec).astype(BF16), xdt_b[:, ps]))
            yd = jnp.where(lane64, halves[0], halves[1])
            y_pairs.append(yd + y_off[:, pm * LANES:(pm + 1) * LANES] + dskip_ref[:, ps] * xs[:, ps])
        yg = jnp.concatenate(y_pairs, axis=1) * _silu(proj_ref[:, gcs])
        yg = yg * lax.rsqrt(jnp.mean(yg * yg, axis=-1, keepdims=True) + EPS)
        y_ref[:, gcs] = (yg * normw_ref[:, gcs]).astype(BF16)

    @pl.when(c == nc - 1)
    def _final_state():
        for t in range(n_tr):
            ssmn_ref[0, 2 * t:2 * t + 2] = ht_s[:, t * LANES:(t + 1) * LANES].T.reshape(
                2, SSM_HEAD_DIM, SSM_STATE)

    k_new = proj_ref[:, off_k:off_v]
    v_new = proj_ref[:, off_v:off_g]
    base = pl.multiple_of(lax.rem(c + 2, 3) * LANES, LANES)
    _write_kv_slot(k_new, kbd, base)
    _write_kv_slot(v_new, vbd, base)

    @pl.when(c == 0)
    def _():
        kn_ref[0, 0:CHUNK, :] = k0_ref[0, CHUNK:2 * CHUNK, :]
        vn_ref[0, 0:CHUNK, :] = v0_ref[0, CHUNK:2 * CHUNK, :]

    @pl.when(c != 0)
    def _():
        kn_ref[0, 0:CHUNK, :] = kn_ref[0, CHUNK:2 * CHUNK, :]
        vn_ref[0, 0:CHUNK, :] = vn_ref[0, CHUNK:2 * CHUNK, :]

    kn_ref[0, CHUNK:2 * CHUNK, :] = k_new
    vn_ref[0, CHUNK:2 * CHUNK, :] = v_new

    neg_inf = jnp.float32(-jnp.inf)
    qscale = ATT_HEAD_DIM ** -0.5
    for kh in range(ATT_KV_HEADS):
        qa = slice(off_q + 2 * kh * LANES, off_q + (2 * kh + 1) * LANES)
        qb = slice(off_q + (2 * kh + 1) * LANES, off_q + (2 * kh + 2) * LANES)
        qs = (jnp.concatenate([proj_ref[:, qa], proj_ref[:, qb]], axis=0) * qscale).astype(BF16)
        s = _dot_nt(qs, kbd[kh])
        sb = [s[:, t * LANES:(t + 1) * LANES] for t in range(3)]
        if first:
            sb[0] = jnp.where(c == 0, neg_inf, sb[0])
            sb[1] = jnp.where(c <= 1, neg_inf, sb[1])
        mb = jnp.maximum(jnp.maximum(sb[0], sb[1]), sb[2])
        m_lo = jnp.max(jnp.where(lane128, mb, neg_inf), axis=-1, keepdims=True)
        m_hi = jnp.max(jnp.where(lane128, neg_inf, mb), axis=-1, keepdims=True)
        sink = jnp.concatenate([jnp.broadcast_to(sink_ref[kh, 0:1, :], (CHUNK, LANES)),
                                jnp.broadcast_to(sink_ref[kh, 1:2, :], (CHUNK, LANES))], axis=0)
        mx = jnp.maximum(jnp.where(lane128, m_lo, m_hi), sink)
        p = jnp.concatenate([jnp.exp(x - mx) for x in sb], axis=1).astype(BF16)
        ov = _dot(p, vbd[kh])
        denom = ov[:, LANES:2 * LANES] + jnp.exp(sink - mx)
        o = ov[:, 0:LANES] / denom
        ga = slice(off_g + 2 * kh * LANES, off_g + (2 * kh + 1) * LANES)
        gb = slice(off_g + (2 * kh + 1) * LANES, off_g + (2 * kh + 2) * LANES)
        ya = slice(inner + 2 * kh * LANES, inner + (2 * kh + 1) * LANES)
        yb = slice(inner + (2 * kh + 1) * LANES, inner + (2 * kh + 2) * LANES)
        y_ref[:, ya] = (o[0:CHUNK] * _silu(proj_ref[:, ga])).astype(BF16)
        y_ref[:, yb] = (o[CHUNK:2 * CHUNK] * _silu(proj_ref[:, gb])).astype(BF16)


def _even_mixer_call(first, proj, pdt, conv0, ssm0, k0, v0, convw, convb, dtb, alog, dskip_x, normw,
                     sink_t, expand, nbatch, seq_len):
    nc = seq_len // CHUNK
    nproj = proj.shape[1]
    inner = normw.shape[1]
    conv_dim = convw.shape[1]
    gn = (conv_dim - inner) // 2
    kvd = k0.shape[2]
    n_heads = inner // SSM_HEAD_DIM
    row = lambda b, c: (b * nc + c, 0)
    per_b3 = lambda b, c: (b, 0, 0)
    per_b4 = lambda b, c: (b, 0, 0, 0)
    const2 = lambda b, c: (0, 0)
    const3 = lambda b, c: (0, 0, 0)
    return pl.pallas_call(
        functools.partial(_even_mixer_kernel, first, (inner, gn, kvd)),
        grid=(nbatch, nc),
        in_specs=[
            pl.BlockSpec((CHUNK, nproj), row),
            pl.BlockSpec((CHUNK, LANES), row),
            pl.BlockSpec((1, SSM_CONV - 1, conv_dim), per_b3),
            pl.BlockSpec((1, n_heads, SSM_HEAD_DIM, SSM_STATE), per_b4),
            pl.BlockSpec((1, WINDOW, kvd), per_b3),
            pl.BlockSpec((1, WINDOW, kvd), per_b3),
            pl.BlockSpec((SSM_CONV, conv_dim), const2),
            pl.BlockSpec((1, conv_dim), const2),
            pl.BlockSpec((1, LANES), const2),
            pl.BlockSpec((1, LANES), const2),
            pl.BlockSpec((1, inner), const2),
            pl.BlockSpec((1, inner), const2),
            pl.BlockSpec((ATT_KV_HEADS, 2, LANES), const3),
            pl.BlockSpec((LANES, inner), const2),
        ],
        out_specs=[
            pl.BlockSpec((CHUNK, 2 * inner), row),
            pl.BlockSpec((1, SSM_CONV - 1, conv_dim), per_b3),
            pl.BlockSpec((1, n_heads, SSM_HEAD_DIM, SSM_STATE), per_b4),
            pl.BlockSpec((1, WINDOW, kvd), per_b3),
            pl.BlockSpec((1, WINDOW, kvd), per_b3),
        ],
        out_shape=[
            jax.ShapeDtypeStruct((nbatch * seq_len, 2 * inner), BF16),
            jax.ShapeDtypeStruct((nbatch, SSM_CONV - 1, conv_dim), F32),
            jax.ShapeDtypeStruct((nbatch, n_heads, SSM_HEAD_DIM, SSM_STATE), F32),
            jax.ShapeDtypeStruct((nbatch, WINDOW, kvd), F32),
            jax.ShapeDtypeStruct((nbatch, WINDOW, kvd), F32),
        ],
        scratch_shapes=[
            pltpu.VMEM((CHUNK + 8, conv_dim), F32),
            pltpu.VMEM((CHUNK, inner), F32),
            pltpu.VMEM((CHUNK, 2 * gn), F32),
            pltpu.VMEM((SSM_STATE, inner), F32),
            pltpu.VMEM((ATT_KV_HEADS, 3 * LANES, LANES), BF16),
            pltpu.VMEM((ATT_KV_HEADS, 3 * LANES, 2 * LANES), BF16),
        ],
        compiler_params=_cparams(("arbitrary", "arbitrary")),
        name="even_mixer",
    )(proj, pdt, conv0, ssm0, k0, v0, convw, convb, dtb, alog, dskip_x, normw, sink_t, expand)


def _pad_lanes(v):
    return jnp.pad(v, ((0, 0), (0, LANES - v.shape[1])))


def kernel(x_prompt, x_sample, c_prompt, c_sample, cache_k, cache_v, state_conv_a, state_ssm, state_conv_c,
           w_ada, b_ada, norm_pre, norm_post, w_in_even, conv_a_w, conv_a_b, dt_bias, a_log, d_skip, norm_ssm,
           sinks, w_out_even, w_in_odd, conv_c_w, w_out_odd):
    bp, lp, d = x_prompt.shape
    bs, ls, _ = x_sample.shape
    depth = w_ada.shape[0]
    inner = norm_ssm.shape[1]
    n_heads = dt_bias.shape[1]
    conv_dim = conv_a_w.shape[2]
    kvd = cache_k.shape[3] * cache_k.shape[4]
    width = conv_c_w.shape[2]
    assert ls == CHUNK and lp % CHUNK == 0 and inner == d and n_heads <= LANES

    mod = _mod_call(jnp.concatenate([c_prompt, c_sample], axis=0), w_ada, b_ada)

    xp = x_prompt.reshape(bp * lp, d)
    xs = x_sample.reshape(bs * ls, d)
    tm_p = min(TM_IN, lp)
    tm_s = bs * ls
    tm_out_p = min(TM_OUT, lp)
    expand = (jnp.arange(LANES)[:, None] == (jnp.arange(inner) // SSM_HEAD_DIM)[None, :]).astype(BF16)
    zeros_p = dict(
        conv_a=jnp.zeros((bp, SSM_CONV - 1, conv_dim), F32),
        ssm=jnp.zeros((bp, n_heads, SSM_HEAD_DIM, SSM_STATE), F32),
        kv=jnp.zeros((bp, WINDOW, kvd), F32),
        conv_c=jnp.zeros((bp, SHORT_CONV - 1, width), F32),
    )
    outs_p = dict(k=[], v=[], ca=[], ssm=[], cc=[])
    outs_s = dict(k=[], v=[], ca=[], ssm=[], cc=[])
    dt_lo = inner + conv_dim
    dt_hi = dt_lo + n_heads
    tc = TC_ODD

    for layer in range(depth):
        i = layer // 2
        shift, scale, gate = jnp.split(mod[layer], 3, axis=-1)
        rep = lambda v: jnp.repeat(v[:bp], lp // CHUNK, axis=0)
        g_pre = norm_pre[layer][None, :]
        g_post = norm_post[layer][None, :]
        if layer % 2 == 0:
            w = w_in_even[i]
            w_main = jnp.concatenate([w[:, :dt_lo], w[:, dt_hi:]], axis=1).astype(BF16)
            w_dt = _pad_lanes(w[:, dt_lo:dt_hi]).astype(BF16)
            w_out = w_out_even[i].astype(BF16)
            mixer_params = (
                conv_a_w[i], conv_a_b[i][None, :], _pad_lanes(dt_bias[i][None, :]), _pad_lanes(a_log[i][None, :]),
                jnp.repeat(d_skip[i], SSM_HEAD_DIM)[None, :], norm_ssm[i][None, :],
                jnp.repeat(sinks[i], ATT_HEAD_DIM).reshape(ATT_KV_HEADS, 2, LANES), expand)
            proj, pdt = _inproj_even_call(xp, rep(scale), rep(shift), g_pre, w_main, w_dt, tm_p, TN_IN)
            y, ca, ssm, kn, vn = _even_mixer_call(True, proj, pdt, zeros_p["conv_a"], zeros_p["ssm"],
                                                  zeros_p["kv"], zeros_p["kv"], *mixer_params, bp, lp)
            xp = _outproj_call(y, w_out, xp, rep(gate), g_post, tm_out_p)
            outs_p["k"].append(kn); outs_p["v"].append(vn); outs_p["ca"].append(ca); outs_p["ssm"].append(ssm)
            proj, pdt = _inproj_even_call(xs, scale[bp:], shift[bp:], g_pre, w_main, w_dt, tm_s, TN_IN)
            y, ca, ssm, kn, vn = _even_mixer_call(False, proj, pdt, state_conv_a[i], state_ssm[i],
                                                  cache_k[i].reshape(bs, WINDOW, kvd),
                                                  cache_v[i].reshape(bs, WINDOW, kvd), *mixer_params, bs, ls)
            xs = _outproj_call(y, w_out, xs, gate[bp:], g_post, tm_s)
            outs_s["k"].append(kn); outs_s["v"].append(vn); outs_s["ca"].append(ca); outs_s["ssm"].append(ssm)
        else:
            w = w_in_odd[i]
            w_r = w.reshape(d, 4, width // tc, tc).transpose(0, 2, 1, 3).reshape(d, 4 * width).astype(BF16)
            w_out = w_out_odd[i].astype(BF16)
            y, cc = _inproj_odd_call(xp, rep(scale), rep(shift), g_pre, w_r, conv_c_w[i], zeros_p["conv_c"],
                                     lp, tm_p, tc)
            xp = _outproj_call(y, w_out, xp, rep(gate), g_post, tm_out_p)
            outs_p["cc"].append(cc)
            y, cc = _inproj_odd_call(xs, scale[bp:], shift[bp:], g_pre, w_r, conv_c_w[i], state_conv_c[i],
                                     ls, tm_s, tc)
            xs = _outproj_call(y, w_out, xs, gate[bp:], g_post, tm_s)
            outs_s["cc"].append(cc)

    kv_shape_p = (len(outs_p["k"]), bp, WINDOW) + cache_k.shape[3:]
    kv_shape_s = (len(outs_s["k"]), bs, WINDOW) + cache_k.shape[3:]
    return (xp.reshape(bp, lp, d), xs.reshape(bs, ls, d),
            jnp.stack(outs_p["k"]).reshape(kv_shape_p), jnp.stack(outs_p["v"]).reshape(kv_shape_p),
            jnp.stack(outs_p["ca"]), jnp.stack(outs_p["ssm"]), jnp.stack(outs_p["cc"]),
            jnp.stack(outs_s["k"]).reshape(kv_shape_s), jnp.stack(outs_s["v"]).reshape(kv_shape_s),
            jnp.stack(outs_s["ca"]), jnp.stack(outs_s["ssm"]), jnp.stack(outs_s["cc"]))
```

```python
import functools

import jax
import jax.numpy as jnp
from jax import lax
from jax.experimental import pallas as pl
from jax.experimental.pallas import tpu as pltpu

F32 = jnp.float32
BF16 = jnp.bfloat16

CHUNK = 64
WINDOW = 128
EPS = 1e-6
SSM_HEAD_DIM = 64
SSM_GROUPS = 4
SSM_STATE = 128
SSM_CONV = 4
ATT_HEAD_DIM = 64
ATT_KV_HEADS = 8
SHORT_CONV = 3
LANES = 128

VMEM_LIMIT = 56 * 1024 * 1024
TM_IN = 1024
TN_IN = 1024
TM_OUT = 512
TC_ODD = 256
SUB_ROWS = 256


def _cparams(sem):
    return pltpu.CompilerParams(dimension_semantics=sem, vmem_limit_bytes=VMEM_LIMIT)


def _dot(a, b):
    return jnp.dot(a, b, preferred_element_type=F32)


def _dot_nt(a, b):
    return lax.dot_general(a, b, (((1,), (1,)), ((), ())), preferred_element_type=F32)


def _dot_tn(a, b):
    return lax.dot_general(a, b, (((0,), (0,)), ((), ())), preferred_element_type=F32)


def _silu(x):
    return x * jax.nn.sigmoid(x)


def _cast_kernel(x_ref, o_ref):
    o_ref[...] = x_ref[...].astype(BF16)


def _cast_call(w, tr):
    n, r, c = w.shape
    return pl.pallas_call(
        _cast_kernel,
        grid=(n, r // tr),
        in_specs=[pl.BlockSpec((None, tr, c), lambda l, i: (l, i, 0))],
        out_specs=pl.BlockSpec((None, tr, c), lambda l, i: (l, i, 0)),
        out_shape=jax.ShapeDtypeStruct(w.shape, BF16),
        compiler_params=_cparams(("arbitrary", "arbitrary")),
        name="cast_bf16",
    )(w)


def _cast_even_kernel(dt_lo, dt_hi, x_ref, o_ref, odt_ref):
    n_main = o_ref.shape[1]
    o_ref[:, 0:dt_lo] = x_ref[:, 0:dt_lo].astype(BF16)
    o_ref[:, dt_lo:n_main] = x_ref[:, dt_hi:dt_hi + n_main - dt_lo].astype(BF16)
    odt_ref[...] = jnp.zeros(odt_ref.shape, BF16)
    odt_ref[:, 0:dt_hi - dt_lo] = x_ref[:, dt_lo:dt_hi].astype(BF16)


def _cast_even_call(w, dt_lo, dt_hi, tr):
    n, r, c = w.shape
    n_main = c - (dt_hi - dt_lo)
    return pl.pallas_call(
        functools.partial(_cast_even_kernel, dt_lo, dt_hi),
        grid=(n, r // tr),
        in_specs=[pl.BlockSpec((None, tr, c), lambda l, i: (l, i, 0))],
        out_specs=[pl.BlockSpec((None, tr, n_main), lambda l, i: (l, i, 0)),
                   pl.BlockSpec((None, tr, LANES), lambda l, i: (l, i, 0))],
        out_shape=[jax.ShapeDtypeStruct((n, r, n_main), BF16),
                   jax.ShapeDtypeStruct((n, r, LANES), BF16)],
        compiler_params=_cparams(("arbitrary", "arbitrary")),
        name="cast_even",
    )(w)


def _mod_kernel(c_ref, w_ref, b_ref, o_ref):
    sc = _silu(c_ref[...]).astype(BF16)
    o_ref[0] = _dot(sc, w_ref[0].astype(BF16)) + b_ref[0]


def _mod_call(c_all, w_ada, b_ada):
    depth, d, d3 = w_ada.shape
    nb = c_all.shape[0]
    tn = 1024
    return pl.pallas_call(
        _mod_kernel,
        grid=(depth, d3 // tn),
        in_specs=[
            pl.BlockSpec((nb, d), lambda l, j: (0, 0)),
            pl.BlockSpec((1, d, tn), lambda l, j: (l, 0, j)),
            pl.BlockSpec((1, 1, tn), lambda l, j: (l, 0, j)),
        ],
        out_specs=pl.BlockSpec((1, nb, tn), lambda l, j: (l, 0, j)),
        out_shape=jax.ShapeDtypeStruct((depth, nb, d3), F32),
        compiler_params=_cparams(("arbitrary", "arbitrary")),
        name="adaln_mod",
    )(c_all, w_ada, b_ada.reshape(depth, 1, d3))


def _modulated_norm_into(x_ref, scale_ref, shift_ref, g_ref, h_scr, tm):
    g = g_ref[...]
    for cc in range(tm // CHUNK):
        rows = slice(cc * CHUNK, (cc + 1) * CHUNK)
        x = x_ref[rows, :]
        ms = jnp.mean(x * x, axis=-1, keepdims=True)
        y = x * lax.rsqrt(ms + EPS) * g
        h = y * (1.0 + scale_ref[cc:cc + 1, :]) + shift_ref[cc:cc + 1, :]
        h_scr[rows, :] = h.astype(BF16)


def _inproj_even_kernel(tm, x_ref, scale_ref, shift_ref, g_ref, w_ref, wdt_ref, o_ref, odt_ref, h_scr):
    @pl.when(pl.program_id(1) == 0)
    def _():
        _modulated_norm_into(x_ref, scale_ref, shift_ref, g_ref, h_scr, tm)
        odt_ref[...] = _dot(h_scr[...], wdt_ref[...])

    o_ref[...] = _dot(h_scr[...], w_ref[...])


def _inproj_even_call(x, scale_pc, shift_pc, g, w_main, w_dt, li, tm, tn):
    r, d = x.shape
    n = w_main.shape[2]
    nch = tm // CHUNK
    return pl.pallas_call(
        functools.partial(_inproj_even_kernel, tm),
        grid=(r // tm, n // tn),
        in_specs=[
            pl.BlockSpec((tm, d), lambda i, j: (i, 0)),
            pl.BlockSpec((nch, d), lambda i, j: (i, 0)),
            pl.BlockSpec((nch, d), lambda i, j: (i, 0)),
            pl.BlockSpec((1, d), lambda i, j: (0, 0)),
            pl.BlockSpec((None, d, tn), lambda i, j: (li, 0, j)),
            pl.BlockSpec((None, d, LANES), lambda i, j: (li, 0, 0)),
        ],
        out_specs=[
            pl.BlockSpec((tm, tn), lambda i, j: (i, j)),
            pl.BlockSpec((tm, LANES), lambda i, j: (i, 0)),
        ],
        out_shape=[
            jax.ShapeDtypeStruct((r, n), F32),
            jax.ShapeDtypeStruct((r, LANES), F32),
        ],
        scratch_shapes=[pltpu.VMEM((tm, d), BF16)],
        compiler_params=_cparams(("arbitrary", "arbitrary")),
        name="inproj_even",
    )(x, scale_pc, shift_pc, g, w_main, w_dt)


def _inproj_odd_kernel(tm, sub, tc, tiles_per_seq, per_chunk_start,
                       x_ref, scale_ref, shift_ref, g_ref, wu_ref, wb_ref, wc_ref, wg_ref, cw_ref, c0_ref,
                       y_ref, cn_ref, h_scr, ebuf, halo):
    i = pl.program_id(0)
    j = pl.program_id(1)

    @pl.when(j == 0)
    def _():
        _modulated_norm_into(x_ref, scale_ref, shift_ref, g_ref, h_scr, tm)

    w0 = cw_ref[0:1, :]
    w1 = cw_ref[1:2, :]
    w2 = cw_ref[2:3, :]
    if not per_chunk_start:
        @pl.when(i % tiles_per_seq == 0)
        def _():
            ebuf[CHUNK + 6:CHUNK + 8, :] = c0_ref[0]

        @pl.when(i % tiles_per_seq != 0)
        def _():
            ebuf[CHUNK + 6:CHUNK + 8, :] = halo[j, 6:8, :]

    for sb in range(tm // sub):
        hs = h_scr[sb * sub:(sb + 1) * sub, :]
        pu = _dot(hs, wu_ref[...])
        pb = _dot(hs, wb_ref[...])
        pc = _dot(hs, wc_ref[...])
        pg = _dot(hs, wg_ref[...])
        for c2 in range(sub // CHUNK):
            cc = sb * (sub // CHUNK) + c2
            r2 = slice(c2 * CHUNK, (c2 + 1) * CHUNK)
            rows = slice(cc * CHUNK, (cc + 1) * CHUNK)
            if per_chunk_start:
                ebuf[6:8, :] = c0_ref[cc]
            else:
                ebuf[6:8, :] = ebuf[CHUNK + 6:CHUNK + 8, :]
            ebuf[8:8 + CHUNK, :] = pc[r2] * pu[r2]
            conv = w0 * ebuf[6:6 + CHUNK, :] + w1 * ebuf[7:7 + CHUNK, :] + w2 * ebuf[8:8 + CHUNK, :]
            y_ref[rows, :] = (pb[r2] * conv * _silu(pg[r2])).astype(BF16)
            if per_chunk_start:
                cn_ref[cc] = ebuf[CHUNK + 6:CHUNK + 8, :]
    if not per_chunk_start:
        tail = ebuf[CHUNK + 6:CHUNK + 8, :]
        halo[j, 6:8, :] = tail
        cn_ref[0] = tail


def _inproj_odd_call(x, scale_pc, shift_pc, g, w_all, li, conv_w, conv0, seq_len, tm, tc):
    r, d = x.shape
    width = conv_w.shape[1]
    nct = width // tc
    nch = tm // CHUNK
    sub = min(SUB_ROWS, tm)
    per_chunk_start = seq_len == CHUNK
    if per_chunk_start:
        tiles_per_seq = 1
        nb = nch
        c0_map = lambda i, j: (i, 0, j)
    else:
        assert seq_len % tm == 0
        tiles_per_seq = seq_len // tm
        nb = 1
        c0_map = lambda i, j: (i // tiles_per_seq, 0, j)
    wspec = lambda q: pl.BlockSpec((None, d, tc), lambda i, j: (li, 0, q * nct + j))
    y, tails = pl.pallas_call(
        functools.partial(_inproj_odd_kernel, tm, sub, tc, tiles_per_seq, per_chunk_start),
        grid=(r // tm, nct),
        in_specs=[
            pl.BlockSpec((tm, d), lambda i, j: (i, 0)),
            pl.BlockSpec((nch, d), lambda i, j: (i, 0)),
            pl.BlockSpec((nch, d), lambda i, j: (i, 0)),
            pl.BlockSpec((1, d), lambda i, j: (0, 0)),
            wspec(0), wspec(1), wspec(2), wspec(3),
            pl.BlockSpec((SHORT_CONV, tc), lambda i, j: (0, j)),
            pl.BlockSpec((nb, SHORT_CONV - 1, tc), c0_map),
        ],
        out_specs=[
            pl.BlockSpec((tm, tc), lambda i, j: (i, j)),
            pl.BlockSpec((nb, SHORT_CONV - 1, tc), lambda i, j: (i, 0, j)),
        ],
        out_shape=[
            jax.ShapeDtypeStruct((r, width), BF16),
            jax.ShapeDtypeStruct((r // tm * nb, SHORT_CONV - 1, width), F32),
        ],
        scratch_shapes=[
            pltpu.VMEM((tm, d), BF16),
            pltpu.VMEM((CHUNK + 8, tc), F32),
            pltpu.VMEM((nct, 8, tc), F32),
        ],
        compiler_params=_cparams(("arbitrary", "arbitrary")),
        name="inproj_odd",
    )(x, scale_pc, shift_pc, g, w_all, w_all, w_all, w_all, conv_w, conv0)
    return y, tails[tiles_per_seq - 1::tiles_per_seq]


def _outproj_kernel(tm, sub, y_ref, w_ref, x_ref, gate_ref, np_ref, o_ref):
    npw = np_ref[...]
    for sb in range(tm // sub):
        o = _dot(y_ref[sb * sub:(sb + 1) * sub, :], w_ref[...])
        for c2 in range(sub // CHUNK):
            cc = sb * (sub // CHUNK) + c2
            rows = slice(cc * CHUNK, (cc + 1) * CHUNK)
            oc = o[c2 * CHUNK:(c2 + 1) * CHUNK, :]
            ms = jnp.mean(oc * oc, axis=-1, keepdims=True)
            r = oc * lax.rsqrt(ms + EPS) * npw
            o_ref[rows, :] = x_ref[rows, :] + gate_ref[cc:cc + 1, :] * r


def _outproj_call(y, w_all, li, x, gate_pc, npost, tm):
    r, k = y.shape
    d = w_all.shape[2]
    nch = tm // CHUNK
    sub = min(SUB_ROWS, tm)
    return pl.pallas_call(
        functools.partial(_outproj_kernel, tm, sub),
        grid=(r // tm,),
        in_specs=[
            pl.BlockSpec((tm, k), lambda i: (i, 0)),
            pl.BlockSpec((None, k, d), lambda i: (li, 0, 0), pipeline_mode=pl.Buffered(1)),
            pl.BlockSpec((tm, d), lambda i: (i, 0)),
            pl.BlockSpec((nch, d), lambda i: (i, 0)),
            pl.BlockSpec((1, d), lambda i: (0, 0)),
        ],
        out_specs=pl.BlockSpec((tm, d), lambda i: (i, 0)),
        out_shape=jax.ShapeDtypeStruct((r, d), F32),
        compiler_params=_cparams(("arbitrary",)),
        name="outproj",
    )(y, w_all, x, gate_pc, npost)


def _write_kv_slot(kv, buf, base):
    lane = lax.broadcasted_iota(jnp.int32, (CHUNK, LANES), 1)
    lo = lane < ATT_HEAD_DIM
    for m in range(kv.shape[1] // LANES):
        blk = kv[:, m * LANES:(m + 1) * LANES]
        sw = pltpu.roll(blk, ATT_HEAD_DIM, 1)
        pieces = (
            (2 * m, jnp.where(lo, blk, 0.0), jnp.where(lo, 0.0, sw)),
            (2 * m + 1, jnp.where(lo, sw, 0.0), jnp.where(lo, 0.0, blk)),
        )
        for head, low, high in pieces:
            buf[head, pl.ds(base, CHUNK), 0:LANES] = low.astype(BF16)
            buf[head, pl.ds(base + CHUNK, CHUNK), 0:LANES] = high.astype(BF16)


def _even_mixer_kernel(first, dims,
                       proj_ref, pdt_ref, conv0_ref, ssm0_ref, k0_ref, v0_ref,
                       convw_ref, convb_ref, dtb_ref, alog_ref, dskip_ref, normw_ref, sink_ref, e_ref,
                       y_ref, convn_ref, ssmn_ref, kn_ref, vn_ref,
                       xext, xs_s, bc_s, ht_s, kbd, vbd):
    inner, gn, kvd = dims
    c = pl.program_id(1)
    nc = pl.num_programs(1)
    n_heads = inner // SSM_HEAD_DIM
    gw = inner // SSM_GROUPS
    off_xbc = inner
    off_q = off_xbc + inner + 2 * gn
    off_k = off_q + inner
    off_v = off_k + kvd
    off_g = off_v + kvd
    n_tr = inner // LANES

    lane64 = lax.broadcasted_iota(jnp.int32, (CHUNK, LANES), 1) < ATT_HEAD_DIM
    lane128 = lax.broadcasted_iota(jnp.int32, (2 * CHUNK, LANES), 1) < ATT_HEAD_DIM

    @pl.when(c == 0)
    def _init():
        xext[5:8, :] = conv0_ref[0]
        for t in range(n_tr):
            ht_s[:, t * LANES:(t + 1) * LANES] = ssm0_ref[0, 2 * t:2 * t + 2].reshape(LANES, SSM_STATE).T
        for s in range(WINDOW // CHUNK):
            _write_kv_slot(k0_ref[0, s * CHUNK:(s + 1) * CHUNK, :], kbd, s * LANES)
            _write_kv_slot(v0_ref[0, s * CHUNK:(s + 1) * CHUNK, :], vbd, s * LANES)
        rowi = lax.broadcasted_iota(jnp.int32, (3 * LANES, LANES), 0)
        lanei = lax.broadcasted_iota(jnp.int32, (3 * LANES, LANES), 1)
        ones_bd = jnp.where((rowi % LANES < CHUNK) == (lanei < ATT_HEAD_DIM), 1.0, 0.0).astype(BF16)
        for h in range(ATT_KV_HEADS):
            vbd[h, :, LANES:2 * LANES] = ones_bd

    xext[8:8 + CHUNK, :] = proj_ref[:, off_xbc:off_q]
    piece = 512
    for t in range((inner + 2 * gn) // piece):
        cs = slice(t * piece, (t + 1) * piece)
        acc = convb_ref[:, cs] + convw_ref[3:4, cs] * xext[8:8 + CHUNK, cs]
        for i in range(SSM_CONV - 1):
            acc = acc + convw_ref[i:i + 1, cs] * xext[5 + i:5 + i + CHUNK, cs]
        act = _silu(acc)
        if t * piece < inner:
            xs_s[:, cs] = act
        else:
            bc_s[:, t * piece - inner:(t + 1) * piece - inner] = act
    tail = xext[CHUNK + 5:CHUNK + 8, :]
    convn_ref[0] = tail
    xext[5:8, :] = tail

    dt = jax.nn.softplus(pdt_ref[...] + dtb_ref[...])
    da = dt * (-jnp.exp(alog_ref[...]))
    ri = lax.broadcasted_iota(jnp.int32, (CHUNK, CHUNK), 0)
    ci = lax.broadcasted_iota(jnp.int32, (CHUNK, CHUNK), 1)
    causal = ri >= ci
    tri = jnp.where(causal, 1.0, 0.0).astype(BF16)
    d_hi = da.astype(BF16)
    r1 = da - d_hi.astype(F32)
    d_mid = r1.astype(BF16)
    d_lo = (r1 - d_mid.astype(F32)).astype(BF16)
    a_cum = _dot(tri, d_hi) + _dot(tri, d_mid) + _dot(tri, d_lo)
    a_last = a_cum[CHUNK - 1:CHUNK, :]
    stack = jnp.concatenate([jnp.exp(a_cum), dt, jnp.exp(a_last - a_cum)], axis=0)
    s_hi = stack.astype(BF16)
    s_lo = (stack - s_hi.astype(F32)).astype(BF16)
    ex = _dot(s_hi, e_ref[...]) + _dot(s_lo, e_ref[...])
    ea = ex[0:CHUNK]
    xs = xs_s[...]
    xdt = xs * ex[CHUNK:2 * CHUNK]
    xdt_b = xdt.astype(BF16)
    xsc_b = (xdt * ex[2 * CHUNK:3 * CHUNK]).astype(BF16)
    act_t = jnp.concatenate([a_cum, jnp.zeros((CHUNK, LANES), F32)], axis=0).T

    pairs_per_group = gw // LANES
    for g in range(SSM_GROUPS):
        gcs = slice(g * gw, (g + 1) * gw)
        bg = bc_s[:, g * SSM_STATE:(g + 1) * SSM_STATE].astype(BF16)
        cg = bc_s[:, gn + g * SSM_STATE:gn + (g + 1) * SSM_STATE].astype(BF16)
        cb = _dot_nt(cg, bg)
        h_prev = ht_s[:, gcs]
        y_off = _dot(cg, h_prev.astype(BF16)) * ea[:, gcs]
        ht_s[:, gcs] = h_prev * ea[CHUNK - 1:CHUNK, gcs] + _dot_tn(bg, xsc_b[:, gcs])
        y_pairs = []
        for pm in range(pairs_per_group):
            m = g * pairs_per_group + pm
            ps = slice(m * LANES, (m + 1) * LANES)
            halves = []
            for e in range(2):
                h = 2 * m + e
                seg = a_cum[:, h:h + 1] - act_t[h:h + 1, 0:CHUNK]
                dec = jnp.exp(jnp.where(causal, seg, -jnp.inf))
                halves.append(_dot((cb * dec).astype(BF16), xdt_b[:, ps]))
            yd = jnp.where(lane64, halves[0], halves[1])
            y_pairs.append(yd + y_off[:, pm * LANES:(pm + 1) * LANES] + dskip_ref[:, ps] * xs[:, ps])
        yg = jnp.concatenate(y_pairs, axis=1) * _silu(proj_ref[:, gcs])
        yg = yg * lax.rsqrt(jnp.mean(yg * yg, axis=-1, keepdims=True) + EPS)
        y_ref[:, gcs] = (yg * normw_ref[:, gcs]).astype(BF16)

    @pl.when(c == nc - 1)
    def _final_state():
        for t in range(n_tr):
            ssmn_ref[0, 2 * t:2 * t + 2] = ht_s[:, t * LANES:(t + 1) * LANES].T.reshape(
                2, SSM_HEAD_DIM, SSM_STATE)

    k_new = proj_ref[:, off_k:off_v]
    v_new = proj_ref[:, off_v:off_g]
    base = pl.multiple_of(lax.rem(c + 2, 3) * LANES, LANES)
    _write_kv_slot(k_new, kbd, base)
    _write_kv_slot(v_new, vbd, base)

    @pl.when(c == 0)
    def _():
        kn_ref[0, 0:CHUNK, :] = k0_ref[0, CHUNK:2 * CHUNK, :]
        vn_ref[0, 0:CHUNK, :] = v0_ref[0, CHUNK:2 * CHUNK, :]

    @pl.when(c != 0)
    def _():
        kn_ref[0, 0:CHUNK, :] = kn_ref[0, CHUNK:2 * CHUNK, :]
        vn_ref[0, 0:CHUNK, :] = vn_ref[0, CHUNK:2 * CHUNK, :]

    kn_ref[0, CHUNK:2 * CHUNK, :] = k_new
    vn_ref[0, CHUNK:2 * CHUNK, :] = v_new

    neg_inf = jnp.float32(-jnp.inf)
    qscale = ATT_HEAD_DIM ** -0.5
    for kh in range(ATT_KV_HEADS):
        qa = slice(off_q + 2 * kh * LANES, off_q + (2 * kh + 1) * LANES)
        qb = slice(off_q + (2 * kh + 1) * LANES, off_q + (2 * kh + 2) * LANES)
        qs = (jnp.concatenate([proj_ref[:, qa], proj_ref[:, qb]], axis=0) * qscale).astype(BF16)
        s = _dot_nt(qs, kbd[kh])
        sb = [s[:, t * LANES:(t + 1) * LANES] for t in range(3)]
        if first:
            sb[0] = jnp.where(c == 0, neg_inf, sb[0])
            sb[1] = jnp.where(c <= 1, neg_inf, sb[1])
        mb = jnp.maximum(jnp.maximum(sb[0], sb[1]), sb[2])
        m_lo = jnp.max(jnp.where(lane128, mb, neg_inf), axis=-1, keepdims=True)
        m_hi = jnp.max(jnp.where(lane128, neg_inf, mb), axis=-1, keepdims=True)
        sink = jnp.concatenate([jnp.broadcast_to(sink_ref[kh, 0:1, :], (CHUNK, LANES)),
                                jnp.broadcast_to(sink_ref[kh, 1:2, :], (CHUNK, LANES))], axis=0)
        mx = jnp.maximum(jnp.where(lane128, m_lo, m_hi), sink)
        p = jnp.concatenate([jnp.exp(x - mx) for x in sb], axis=1).astype(BF16)
        ov = _dot(p, vbd[kh])
        denom = ov[:, LANES:2 * LANES] + jnp.exp(sink - mx)
        o = ov[:, 0:LANES] / denom
        ga = slice(off_g + 2 * kh * LANES, off_g + (2 * kh + 1) * LANES)
        gb = slice(off_g + (2 * kh + 1) * LANES, off_g + (2 * kh + 2) * LANES)
        ya = slice(inner + 2 * kh * LANES, inner + (2 * kh + 1) * LANES)
        yb = slice(inner + (2 * kh + 1) * LANES, inner + (2 * kh + 2) * LANES)
        y_ref[:, ya] = (o[0:CHUNK] * _silu(proj_ref[:, ga])).astype(BF16)
        y_ref[:, yb] = (o[CHUNK:2 * CHUNK] * _silu(proj_ref[:, gb])).astype(BF16)


def _even_mixer_call(first, proj, pdt, conv0, ssm0, k0, v0, convw, convb, dtb, alog, dskip_x, normw,
                     sink_t, expand, nbatch, seq_len):
    nc = seq_len // CHUNK
    nproj = proj.shape[1]
    inner = normw.shape[1]
    conv_dim = convw.shape[1]
    gn = (conv_dim - inner) // 2
    kvd = k0.shape[2]
    n_heads = inner // SSM_HEAD_DIM
    row = lambda b, c: (b * nc + c, 0)
    per_b3 = lambda b, c: (b, 0, 0)
    per_b4 = lambda b, c: (b, 0, 0, 0)
    const2 = lambda b, c: (0, 0)
    const3 = lambda b, c: (0, 0, 0)
    return pl.pallas_call(
        functools.partial(_even_mixer_kernel, first, (inner, gn, kvd)),
        grid=(nbatch, nc),
        in_specs=[
            pl.BlockSpec((CHUNK, nproj), row),
            pl.BlockSpec((CHUNK, LANES), row),
            pl.BlockSpec((1, SSM_CONV - 1, conv_dim), per_b3),
            pl.BlockSpec((1, n_heads, SSM_HEAD_DIM, SSM_STATE), per_b4),
            pl.BlockSpec((1, WINDOW, kvd), per_b3),
            pl.BlockSpec((1, WINDOW, kvd), per_b3),
            pl.BlockSpec((SSM_CONV, conv_dim), const2),
            pl.BlockSpec((1, conv_dim), const2),
            pl.BlockSpec((1, LANES), const2),
            pl.BlockSpec((1, LANES), const2),
            pl.BlockSpec((1, inner), const2),
            pl.BlockSpec((1, inner), const2),
            pl.BlockSpec((ATT_KV_HEADS, 2, LANES), const3),
            pl.BlockSpec((LANES, inner), const2),
        ],
        out_specs=[
            pl.BlockSpec((CHUNK, 2 * inner), row),
            pl.BlockSpec((1, SSM_CONV - 1, conv_dim), per_b3),
            pl.BlockSpec((1, n_heads, SSM_HEAD_DIM, SSM_STATE), per_b4),
            pl.BlockSpec((1, WINDOW, kvd), per_b3),
            pl.BlockSpec((1, WINDOW, kvd), per_b3),
        ],
        out_shape=[
            jax.ShapeDtypeStruct((nbatch * seq_len, 2 * inner), BF16),
            jax.ShapeDtypeStruct((nbatch, SSM_CONV - 1, conv_dim), F32),
            jax.ShapeDtypeStruct((nbatch, n_heads, SSM_HEAD_DIM, SSM_STATE), F32),
            jax.ShapeDtypeStruct((nbatch, WINDOW, kvd), F32),
            jax.ShapeDtypeStruct((nbatch, WINDOW, kvd), F32),
        ],
        scratch_shapes=[
            pltpu.VMEM((CHUNK + 8, conv_dim), F32),
            pltpu.VMEM((CHUNK, inner), F32),
            pltpu.VMEM((CHUNK, 2 * gn), F32),
            pltpu.VMEM((SSM_STATE, inner), F32),
            pltpu.VMEM((ATT_KV_HEADS, 3 * LANES, LANES), BF16),
            pltpu.VMEM((ATT_KV_HEADS, 3 * LANES, 2 * LANES), BF16),
        ],
        compiler_params=_cparams(("arbitrary", "arbitrary")),
        name="even_mixer",
    )(proj, pdt, conv0, ssm0, k0, v0, convw, convb, dtb, alog, dskip_x, normw, sink_t, expand)


def _pad_lanes(v):
    return jnp.pad(v, ((0, 0), (0, LANES - v.shape[1])))


def kernel(x_prompt, x_sample, c_prompt, c_sample, cache_k, cache_v, state_conv_a, state_ssm, state_conv_c,
           w_ada, b_ada, norm_pre, norm_post, w_in_even, conv_a_w, conv_a_b, dt_bias, a_log, d_skip, norm_ssm,
           sinks, w_out_even, w_in_odd, conv_c_w, w_out_odd):
    bp, lp, d = x_prompt.shape
    bs, ls, _ = x_sample.shape
    depth = w_ada.shape[0]
    inner = norm_ssm.shape[1]
    n_heads = dt_bias.shape[1]
    conv_dim = conv_a_w.shape[2]
    kvd = cache_k.shape[3] * cache_k.shape[4]
    width = conv_c_w.shape[2]
    assert ls == CHUNK and lp % CHUNK == 0 and inner == d and n_heads <= LANES

    mod = _mod_call(jnp.concatenate([c_prompt, c_sample], axis=0), w_ada, b_ada)

    xp = x_prompt.reshape(bp * lp, d)
    xs = x_sample.reshape(bs * ls, d)
    tm_p = min(TM_IN, lp)
    tm_s = bs * ls
    tm_out_p = min(TM_OUT, lp)
    expand = (jnp.arange(LANES)[:, None] == (jnp.arange(inner) // SSM_HEAD_DIM)[None, :]).astype(BF16)
    zeros_p = dict(
        conv_a=jnp.zeros((bp, SSM_CONV - 1, conv_dim), F32),
        ssm=jnp.zeros((bp, n_heads, SSM_HEAD_DIM, SSM_STATE), F32),
        kv=jnp.zeros((bp, WINDOW, kvd), F32),
        conv_c=jnp.zeros((bp, SHORT_CONV - 1, width), F32),
    )
    outs_p = dict(k=[], v=[], ca=[], ssm=[], cc=[])
    outs_s = dict(k=[], v=[], ca=[], ssm=[], cc=[])
    dt_lo = inner + conv_dim
    dt_hi = dt_lo + n_heads
    tc = TC_ODD
    w_main_all, w_dt_all = _cast_even_call(w_in_even, dt_lo, dt_hi, 128)
    w_in_odd_b = _cast_call(w_in_odd, 128)
    w_out_even_b = _cast_call(w_out_even, 512)
    w_out_odd_b = _cast_call(w_out_odd, 512)

    for layer in range(depth):
        i = layer // 2
        shift, scale, gate = jnp.split(mod[layer], 3, axis=-1)
        rep = lambda v: jnp.repeat(v[:bp], lp // CHUNK, axis=0)
        g_pre = norm_pre[layer][None, :]
        g_post = norm_post[layer][None, :]
        if layer % 2 == 0:
            mixer_params = (
                conv_a_w[i], conv_a_b[i][None, :], _pad_lanes(dt_bias[i][None, :]), _pad_lanes(a_log[i][None, :]),
                jnp.repeat(d_skip[i], SSM_HEAD_DIM)[None, :], norm_ssm[i][None, :],
                jnp.repeat(sinks[i], ATT_HEAD_DIM).reshape(ATT_KV_HEADS, 2, LANES), expand)
            proj, pdt = _inproj_even_call(xp, rep(scale), rep(shift), g_pre, w_main_all, w_dt_all, i, tm_p, TN_IN)
            y, ca, ssm, kn, vn = _even_mixer_call(True, proj, pdt, zeros_p["conv_a"], zeros_p["ssm"],
                                                  zeros_p["kv"], zeros_p["kv"], *mixer_params, bp, lp)
            xp = _outproj_call(y, w_out_even_b, i, xp, rep(gate), g_post, tm_out_p)
            outs_p["k"].append(kn); outs_p["v"].append(vn); outs_p["ca"].append(ca); outs_p["ssm"].append(ssm)
            proj, pdt = _inproj_even_call(xs, scale[bp:], shift[bp:], g_pre, w_main_all, w_dt_all, i, tm_s, TN_IN)
            y, ca, ssm, kn, vn = _even_mixer_call(False, proj, pdt, state_conv_a[i], state_ssm[i],
                                                  cache_k[i].reshape(bs, WINDOW, kvd),
                                                  cache_v[i].reshape(bs, WINDOW, kvd), *mixer_params, bs, ls)
            xs = _outproj_call(y, w_out_even_b, i, xs, gate[bp:], g_post, tm_s)
            outs_s["k"].append(kn); outs_s["v"].append(vn); outs_s["ca"].append(ca); outs_s["ssm"].append(ssm)
        else:
            y, cc = _inproj_odd_call(xp, rep(scale), rep(shift), g_pre, w_in_odd_b, i, conv_c_w[i],
                                     zeros_p["conv_c"], lp, tm_p, tc)
            xp = _outproj_call(y, w_out_odd_b, i, xp, rep(gate), g_post, tm_out_p)
            outs_p["cc"].append(cc)
            y, cc = _inproj_odd_call(xs, scale[bp:], shift[bp:], g_pre, w_in_odd_b, i, conv_c_w[i],
                                     state_conv_c[i], ls, tm_s, tc)
            xs = _outproj_call(y, w_out_odd_b, i, xs, gate[bp:], g_post, tm_s)
            outs_s["cc"].append(cc)

    kv_shape_p = (len(outs_p["k"]), bp, WINDOW) + cache_k.shape[3:]
    kv_shape_s = (len(outs_s["k"]), bs, WINDOW) + cache_k.shape[3:]
    return (xp.reshape(bp, lp, d), xs.reshape(bs, ls, d),
            jnp.stack(outs_p["k"]).reshape(kv_shape_p), jnp.stack(outs_p["v"]).reshape(kv_shape_p),
            jnp.stack(outs_p["ca"]), jnp.stack(outs_p["ssm"]), jnp.stack(outs_p["cc"]),
            jnp.stack(outs_s["k"]).reshape(kv_shape_s), jnp.stack(outs_s["v"]).reshape(kv_shape_s),
            jnp.stack(outs_s["ca"]), jnp.stack(outs_s["ssm"]), jnp.stack(outs_s["cc"]))
```

```python
import functools

import jax
import jax.numpy as jnp
from jax import lax
from jax.experimental import pallas as pl
from jax.experimental.pallas import tpu as pltpu

F32 = jnp.float32
BF16 = jnp.bfloat16

CHUNK = 64
WINDOW = 128
EPS = 1e-6
SSM_HEAD_DIM = 64
SSM_GROUPS = 4
SSM_STATE = 128
SSM_CONV = 4
ATT_HEAD_DIM = 64
ATT_KV_HEADS = 8
SHORT_CONV = 3
LANES = 128

VMEM_LIMIT = 56 * 1024 * 1024
VMEM_LIMIT_EVEN_IN = 60 * 1024 * 1024
TM_IN = 1024
TN_IN = 2048
TM_OUT = 512
TC_ODD = 512
SUB_ROWS = 256


def _cparams(sem, vmem=VMEM_LIMIT):
    return pltpu.CompilerParams(dimension_semantics=sem, vmem_limit_bytes=vmem)


def _dot(a, b):
    return jnp.dot(a, b, preferred_element_type=F32)


def _dot_nt(a, b):
    return lax.dot_general(a, b, (((1,), (1,)), ((), ())), preferred_element_type=F32)


def _dot_tn(a, b):
    return lax.dot_general(a, b, (((0,), (0,)), ((), ())), preferred_element_type=F32)


def _silu(x):
    return x * jax.nn.sigmoid(x)


def _cast_kernel(x_ref, o_ref):
    o_ref[...] = x_ref[...].astype(BF16)


def _cast_call(w, tr):
    n, r, c = w.shape
    return pl.pallas_call(
        _cast_kernel,
        grid=(n, r // tr),
        in_specs=[pl.BlockSpec((None, tr, c), lambda l, i: (l, i, 0))],
        out_specs=pl.BlockSpec((None, tr, c), lambda l, i: (l, i, 0)),
        out_shape=jax.ShapeDtypeStruct(w.shape, BF16),
        compiler_params=_cparams(("arbitrary", "arbitrary")),
        name="cast_bf16",
    )(w)


def _cast_even_kernel(dt_lo, dt_hi, x_ref, o_ref, odt_ref):
    n_main = o_ref.shape[1]
    o_ref[:, 0:dt_lo] = x_ref[:, 0:dt_lo].astype(BF16)
    o_ref[:, dt_lo:n_main] = x_ref[:, dt_hi:dt_hi + n_main - dt_lo].astype(BF16)
    odt_ref[...] = jnp.zeros(odt_ref.shape, BF16)
    odt_ref[:, 0:dt_hi - dt_lo] = x_ref[:, dt_lo:dt_hi].astype(BF16)


def _cast_even_call(w, dt_lo, dt_hi, tr):
    n, r, c = w.shape
    n_main = c - (dt_hi - dt_lo)
    return pl.pallas_call(
        functools.partial(_cast_even_kernel, dt_lo, dt_hi),
        grid=(n, r // tr),
        in_specs=[pl.BlockSpec((None, tr, c), lambda l, i: (l, i, 0))],
        out_specs=[pl.BlockSpec((None, tr, n_main), lambda l, i: (l, i, 0)),
                   pl.BlockSpec((None, tr, LANES), lambda l, i: (l, i, 0))],
        out_shape=[jax.ShapeDtypeStruct((n, r, n_main), BF16),
                   jax.ShapeDtypeStruct((n, r, LANES), BF16)],
        compiler_params=_cparams(("arbitrary", "arbitrary")),
        name="cast_even",
    )(w)


def _mod_kernel(c_ref, w_ref, b_ref, o_ref):
    sc = _silu(c_ref[...]).astype(BF16)
    o_ref[0] = _dot(sc, w_ref[0].astype(BF16)) + b_ref[0]


def _mod_call(c_all, w_ada, b_ada):
    depth, d, d3 = w_ada.shape
    nb = c_all.shape[0]
    tn = 1024
    return pl.pallas_call(
        _mod_kernel,
        grid=(depth, d3 // tn),
        in_specs=[
            pl.BlockSpec((nb, d), lambda l, j: (0, 0)),
            pl.BlockSpec((1, d, tn), lambda l, j: (l, 0, j)),
            pl.BlockSpec((1, 1, tn), lambda l, j: (l, 0, j)),
        ],
        out_specs=pl.BlockSpec((1, nb, tn), lambda l, j: (l, 0, j)),
        out_shape=jax.ShapeDtypeStruct((depth, nb, d3), F32),
        compiler_params=_cparams(("arbitrary", "arbitrary")),
        name="adaln_mod",
    )(c_all, w_ada, b_ada.reshape(depth, 1, d3))


def _modulated_norm_into(x_ref, scale_ref, shift_ref, g_ref, h_scr, tm):
    g = g_ref[...]
    for cc in range(tm // CHUNK):
        rows = slice(cc * CHUNK, (cc + 1) * CHUNK)
        x = x_ref[rows, :]
        ms = jnp.mean(x * x, axis=-1, keepdims=True)
        y = x * lax.rsqrt(ms + EPS) * g
        h = y * (1.0 + scale_ref[cc:cc + 1, :]) + shift_ref[cc:cc + 1, :]
        h_scr[rows, :] = h.astype(BF16)


def _inproj_even_kernel(tm, x_ref, scale_ref, shift_ref, g_ref, w_ref, wdt_ref, o_ref, odt_ref, h_scr):
    @pl.when(pl.program_id(1) == 0)
    def _():
        _modulated_norm_into(x_ref, scale_ref, shift_ref, g_ref, h_scr, tm)
        odt_ref[...] = _dot(h_scr[...], wdt_ref[...])

    o_ref[...] = _dot(h_scr[...], w_ref[...])


def _inproj_even_call(x, scale_pc, shift_pc, g, w_main, w_dt, li, tm, tn):
    r, d = x.shape
    n = w_main.shape[2]
    nch = tm // CHUNK
    return pl.pallas_call(
        functools.partial(_inproj_even_kernel, tm),
        grid=(r // tm, n // tn),
        in_specs=[
            pl.BlockSpec((tm, d), lambda i, j: (i, 0)),
            pl.BlockSpec((nch, d), lambda i, j: (i, 0)),
            pl.BlockSpec((nch, d), lambda i, j: (i, 0)),
            pl.BlockSpec((1, d), lambda i, j: (0, 0)),
            pl.BlockSpec((None, d, tn), lambda i, j: (li, 0, j)),
            pl.BlockSpec((None, d, LANES), lambda i, j: (li, 0, 0)),
        ],
        out_specs=[
            pl.BlockSpec((tm, tn), lambda i, j: (i, j)),
            pl.BlockSpec((tm, LANES), lambda i, j: (i, 0)),
        ],
        out_shape=[
            jax.ShapeDtypeStruct((r, n), F32),
            jax.ShapeDtypeStruct((r, LANES), F32),
        ],
        scratch_shapes=[pltpu.VMEM((tm, d), BF16)],
        compiler_params=_cparams(("arbitrary", "arbitrary"), VMEM_LIMIT_EVEN_IN),
        name="inproj_even",
    )(x, scale_pc, shift_pc, g, w_main, w_dt)


def _inproj_odd_kernel(tm, sub, tc, tiles_per_seq, per_chunk_start,
                       x_ref, scale_ref, shift_ref, g_ref, wu_ref, wb_ref, wc_ref, wg_ref, cw_ref, c0_ref,
                       y_ref, cn_ref, h_scr, ebuf, halo):
    i = pl.program_id(0)
    j = pl.program_id(1)

    @pl.when(j == 0)
    def _():
        _modulated_norm_into(x_ref, scale_ref, shift_ref, g_ref, h_scr, tm)

    w0 = cw_ref[0:1, :]
    w1 = cw_ref[1:2, :]
    w2 = cw_ref[2:3, :]
    if not per_chunk_start:
        @pl.when(i % tiles_per_seq == 0)
        def _():
            ebuf[CHUNK + 6:CHUNK + 8, :] = c0_ref[0]

        @pl.when(i % tiles_per_seq != 0)
        def _():
            ebuf[CHUNK + 6:CHUNK + 8, :] = halo[j, 6:8, :]

    for sb in range(tm // sub):
        hs = h_scr[sb * sub:(sb + 1) * sub, :]
        pu = _dot(hs, wu_ref[...])
        pb = _dot(hs, wb_ref[...])
        pc = _dot(hs, wc_ref[...])
        pg = _dot(hs, wg_ref[...])
        for c2 in range(sub // CHUNK):
            cc = sb * (sub // CHUNK) + c2
            r2 = slice(c2 * CHUNK, (c2 + 1) * CHUNK)
            rows = slice(cc * CHUNK, (cc + 1) * CHUNK)
            if per_chunk_start:
                ebuf[6:8, :] = c0_ref[cc]
            else:
                ebuf[6:8, :] = ebuf[CHUNK + 6:CHUNK + 8, :]
            ebuf[8:8 + CHUNK, :] = pc[r2] * pu[r2]
            conv = w0 * ebuf[6:6 + CHUNK, :] + w1 * ebuf[7:7 + CHUNK, :] + w2 * ebuf[8:8 + CHUNK, :]
            y_ref[rows, :] = (pb[r2] * conv * _silu(pg[r2])).astype(BF16)
            if per_chunk_start:
                cn_ref[cc] = ebuf[CHUNK + 6:CHUNK + 8, :]
    if not per_chunk_start:
        tail = ebuf[CHUNK + 6:CHUNK + 8, :]
        halo[j, 6:8, :] = tail
        cn_ref[0] = tail


def _inproj_odd_call(x, scale_pc, shift_pc, g, w_all, li, conv_w, conv0, seq_len, tm, tc):
    r, d = x.shape
    width = conv_w.shape[1]
    nct = width // tc
    nch = tm // CHUNK
    sub = min(SUB_ROWS, tm)
    per_chunk_start = seq_len == CHUNK
    if per_chunk_start:
        tiles_per_seq = 1
        nb = nch
        c0_map = lambda i, j: (i, 0, j)
    else:
        assert seq_len % tm == 0
        tiles_per_seq = seq_len // tm
        nb = 1
        c0_map = lambda i, j: (i // tiles_per_seq, 0, j)
    wspec = lambda q: pl.BlockSpec((None, d, tc), lambda i, j: (li, 0, q * nct + j))
    y, tails = pl.pallas_call(
        functools.partial(_inproj_odd_kernel, tm, sub, tc, tiles_per_seq, per_chunk_start),
        grid=(r // tm, nct),
        in_specs=[
            pl.BlockSpec((tm, d), lambda i, j: (i, 0)),
            pl.BlockSpec((nch, d), lambda i, j: (i, 0)),
            pl.BlockSpec((nch, d), lambda i, j: (i, 0)),
            pl.BlockSpec((1, d), lambda i, j: (0, 0)),
            wspec(0), wspec(1), wspec(2), wspec(3),
            pl.BlockSpec((SHORT_CONV, tc), lambda i, j: (0, j)),
            pl.BlockSpec((nb, SHORT_CONV - 1, tc), c0_map),
        ],
        out_specs=[
            pl.BlockSpec((tm, tc), lambda i, j: (i, j)),
            pl.BlockSpec((nb, SHORT_CONV - 1, tc), lambda i, j: (i, 0, j)),
        ],
        out_shape=[
            jax.ShapeDtypeStruct((r, width), BF16),
            jax.ShapeDtypeStruct((r // tm * nb, SHORT_CONV - 1, width), F32),
        ],
        scratch_shapes=[
            pltpu.VMEM((tm, d), BF16),
            pltpu.VMEM((CHUNK + 8, tc), F32),
            pltpu.VMEM((nct, 8, tc), F32),
        ],
        compiler_params=_cparams(("arbitrary", "arbitrary")),
        name="inproj_odd",
    )(x, scale_pc, shift_pc, g, w_all, w_all, w_all, w_all, conv_w, conv0)
    return y, tails[tiles_per_seq - 1::tiles_per_seq]


def _outproj_kernel(tm, sub, y_ref, w_ref, x_ref, gate_ref, np_ref, o_ref):
    npw = np_ref[...]
    for sb in range(tm // sub):
        o = _dot(y_ref[sb * sub:(sb + 1) * sub, :], w_ref[...])
        for c2 in range(sub // CHUNK):
            cc = sb * (sub // CHUNK) + c2
            rows = slice(cc * CHUNK, (cc + 1) * CHUNK)
            oc = o[c2 * CHUNK:(c2 + 1) * CHUNK, :]
            ms = jnp.mean(oc * oc, axis=-1, keepdims=True)
            r = oc * lax.rsqrt(ms + EPS) * npw
            o_ref[rows, :] = x_ref[rows, :] + gate_ref[cc:cc + 1, :] * r


def _outproj_call(y, w_all, li, x, gate_pc, npost, tm):
    r, k = y.shape
    d = x.shape[1]
    nch = tm // CHUNK
    sub = min(SUB_ROWS, tm)
    return pl.pallas_call(
        functools.partial(_outproj_kernel, tm, sub),
        grid=(r // tm,),
        in_specs=[
            pl.BlockSpec((tm, k), lambda i: (i, 0)),
            pl.BlockSpec((None, k, d), lambda i: (li, 0, 0), pipeline_mode=pl.Buffered(1)),
            pl.BlockSpec((tm, d), lambda i: (i, 0)),
            pl.BlockSpec((nch, d), lambda i: (i, 0)),
            pl.BlockSpec((1, d), lambda i: (0, 0)),
        ],
        out_specs=pl.BlockSpec((tm, d), lambda i: (i, 0)),
        out_shape=jax.ShapeDtypeStruct((r, d), F32),
        compiler_params=_cparams(("arbitrary",)),
        name="outproj",
    )(y, w_all, x, gate_pc, npost)


def _write_kv_slot(kv, buf, base):
    lane = lax.broadcasted_iota(jnp.int32, (CHUNK, LANES), 1)
    lo = lane < ATT_HEAD_DIM
    for m in range(kv.shape[1] // LANES):
        blk = kv[:, m * LANES:(m + 1) * LANES]
        sw = pltpu.roll(blk, ATT_HEAD_DIM, 1)
        pieces = (
            (2 * m, jnp.where(lo, blk, 0.0), jnp.where(lo, 0.0, sw)),
            (2 * m + 1, jnp.where(lo, sw, 0.0), jnp.where(lo, 0.0, blk)),
        )
        for head, low, high in pieces:
            buf[head, pl.ds(base, CHUNK), 0:LANES] = low.astype(BF16)
            buf[head, pl.ds(base + CHUNK, CHUNK), 0:LANES] = high.astype(BF16)


def _mixer_chunk(first, dims, c, nc, write_y,
                 proj_ref, pdt_ref, conv0_ref, ssm0_ref, k0_ref, v0_ref,
                 convw_ref, convb_ref, dtb_ref, alog_ref, dskip_ref, normw_ref, sink_ref, e_ref,
                 convn_ref, ssmn_ref, kn_ref, vn_ref,
                 xext, xs_s, bc_s, ht_s, kbd, vbd):
    inner, gn, kvd = dims
    gw = inner // SSM_GROUPS
    off_xbc = inner
    off_q = off_xbc + inner + 2 * gn
    off_k = off_q + inner
    off_v = off_k + kvd
    off_g = off_v + kvd
    n_tr = inner // LANES

    lane64 = lax.broadcasted_iota(jnp.int32, (CHUNK, LANES), 1) < ATT_HEAD_DIM
    lane128 = lax.broadcasted_iota(jnp.int32, (2 * CHUNK, LANES), 1) < ATT_HEAD_DIM

    @pl.when(c == 0)
    def _init():
        xext[5:8, :] = conv0_ref[0]
        for t in range(n_tr):
            ht_s[:, t * LANES:(t + 1) * LANES] = ssm0_ref[0, 2 * t:2 * t + 2].reshape(LANES, SSM_STATE).T
        for s in range(WINDOW // CHUNK):
            _write_kv_slot(k0_ref[0, s * CHUNK:(s + 1) * CHUNK, :], kbd, s * LANES)
            _write_kv_slot(v0_ref[0, s * CHUNK:(s + 1) * CHUNK, :], vbd, s * LANES)
        rowi = lax.broadcasted_iota(jnp.int32, (3 * LANES, LANES), 0)
        lanei = lax.broadcasted_iota(jnp.int32, (3 * LANES, LANES), 1)
        ones_bd = jnp.where((rowi % LANES < CHUNK) == (lanei < ATT_HEAD_DIM), 1.0, 0.0).astype(BF16)
        for h in range(ATT_KV_HEADS):
            vbd[h, :, LANES:2 * LANES] = ones_bd

    xext[8:8 + CHUNK, :] = proj_ref[:, off_xbc:off_q]
    piece = 512
    for t in range((inner + 2 * gn) // piece):
        cs = slice(t * piece, (t + 1) * piece)
        acc = convb_ref[:, cs] + convw_ref[3:4, cs] * xext[8:8 + CHUNK, cs]
        for i in range(SSM_CONV - 1):
            acc = acc + convw_ref[i:i + 1, cs] * xext[5 + i:5 + i + CHUNK, cs]
        act = _silu(acc)
        if t * piece < inner:
            xs_s[:, cs] = act
        else:
            bc_s[:, t * piece - inner:(t + 1) * piece - inner] = act
    tail = xext[CHUNK + 5:CHUNK + 8, :]
    convn_ref[0] = tail
    xext[5:8, :] = tail

    dt = jax.nn.softplus(pdt_ref[...] + dtb_ref[...])
    da = dt * (-jnp.exp(alog_ref[...]))
    ri = lax.broadcasted_iota(jnp.int32, (CHUNK, CHUNK), 0)
    ci = lax.broadcasted_iota(jnp.int32, (CHUNK, CHUNK), 1)
    causal = ri >= ci
    tri = jnp.where(causal, 1.0, 0.0).astype(BF16)
    d_hi = da.astype(BF16)
    r1 = da - d_hi.astype(F32)
    d_mid = r1.astype(BF16)
    d_lo = (r1 - d_mid.astype(F32)).astype(BF16)
    a_cum = _dot(tri, d_hi) + _dot(tri, d_mid) + _dot(tri, d_lo)
    a_last = a_cum[CHUNK - 1:CHUNK, :]
    stack = jnp.concatenate([jnp.exp(a_cum), dt, jnp.exp(a_last - a_cum)], axis=0)
    s_hi = stack.astype(BF16)
    s_lo = (stack - s_hi.astype(F32)).astype(BF16)
    ex = _dot(s_hi, e_ref[...]) + _dot(s_lo, e_ref[...])
    ea = ex[0:CHUNK]
    xs = xs_s[...]
    xdt = xs * ex[CHUNK:2 * CHUNK]
    xdt_b = xdt.astype(BF16)
    xsc_b = (xdt * ex[2 * CHUNK:3 * CHUNK]).astype(BF16)
    act_t = jnp.concatenate([a_cum, jnp.zeros((CHUNK, LANES), F32)], axis=0).T

    pairs_per_group = gw // LANES

    def ssd_group(g):
        gcs = slice(g * gw, (g + 1) * gw)
        bg = bc_s[:, g * SSM_STATE:(g + 1) * SSM_STATE].astype(BF16)
        cg = bc_s[:, gn + g * SSM_STATE:gn + (g + 1) * SSM_STATE].astype(BF16)
        cb = _dot_nt(cg, bg)
        h_prev = ht_s[:, gcs]
        y_off = _dot(cg, h_prev.astype(BF16)) * ea[:, gcs]
        ht_s[:, gcs] = h_prev * ea[CHUNK - 1:CHUNK, gcs] + _dot_tn(bg, xsc_b[:, gcs])
        y_pairs = []
        for pm in range(pairs_per_group):
            m = g * pairs_per_group + pm
            ps = slice(m * LANES, (m + 1) * LANES)
            halves = []
            for e in range(2):
                h = 2 * m + e
                seg = a_cum[:, h:h + 1] - act_t[h:h + 1, 0:CHUNK]
                dec = jnp.exp(jnp.where(causal, seg, -jnp.inf))
                halves.append(_dot((cb * dec).astype(BF16), xdt_b[:, ps]))
            yd = jnp.where(lane64, halves[0], halves[1])
            y_pairs.append(yd + y_off[:, pm * LANES:(pm + 1) * LANES] + dskip_ref[:, ps] * xs[:, ps])
        yg = jnp.concatenate(y_pairs, axis=1) * _silu(proj_ref[:, gcs])
        yg = yg * lax.rsqrt(jnp.mean(yg * yg, axis=-1, keepdims=True) + EPS)
        write_y(gcs, (yg * normw_ref[:, gcs]).astype(BF16))

    k_new = proj_ref[:, off_k:off_v]
    v_new = proj_ref[:, off_v:off_g]
    base = pl.multiple_of(lax.rem(c + 2, 3) * LANES, LANES)
    _write_kv_slot(k_new, kbd, base)
    _write_kv_slot(v_new, vbd, base)

    neg_inf = jnp.float32(-jnp.inf)
    qscale = ATT_HEAD_DIM ** -0.5

    def attn_head(kh):
        qa = slice(off_q + 2 * kh * LANES, off_q + (2 * kh + 1) * LANES)
        qb = slice(off_q + (2 * kh + 1) * LANES, off_q + (2 * kh + 2) * LANES)
        qs = (jnp.concatenate([proj_ref[:, qa], proj_ref[:, qb]], axis=0) * qscale).astype(BF16)
        s = _dot_nt(qs, kbd[kh])
        sb = [s[:, t * LANES:(t + 1) * LANES] for t in range(3)]
        if first:
            sb[0] = jnp.where(c == 0, neg_inf, sb[0])
            sb[1] = jnp.where(c <= 1, neg_inf, sb[1])
        mb = jnp.maximum(jnp.maximum(sb[0], sb[1]), sb[2])
        m_lo = jnp.max(jnp.where(lane128, mb, neg_inf), axis=-1, keepdims=True)
        m_hi = jnp.max(jnp.where(lane128, neg_inf, mb), axis=-1, keepdims=True)
        sink = jnp.concatenate([jnp.broadcast_to(sink_ref[kh, 0:1, :], (CHUNK, LANES)),
                                jnp.broadcast_to(sink_ref[kh, 1:2, :], (CHUNK, LANES))], axis=0)
        mx = jnp.maximum(jnp.where(lane128, m_lo, m_hi), sink)
        p = jnp.concatenate([jnp.exp(x - mx) for x in sb], axis=1).astype(BF16)
        ov = _dot(p, vbd[kh])
        denom = ov[:, LANES:2 * LANES] + jnp.exp(sink - mx)
        o = ov[:, 0:LANES] / denom
        ga = slice(off_g + 2 * kh * LANES, off_g + (2 * kh + 1) * LANES)
        gb = slice(off_g + (2 * kh + 1) * LANES, off_g + (2 * kh + 2) * LANES)
        ya = slice(inner + 2 * kh * LANES, inner + (2 * kh + 1) * LANES)
        yb = slice(inner + (2 * kh + 1) * LANES, inner + (2 * kh + 2) * LANES)
        write_y(ya, (o[0:CHUNK] * _silu(proj_ref[:, ga])).astype(BF16))
        write_y(yb, (o[CHUNK:2 * CHUNK] * _silu(proj_ref[:, gb])).astype(BF16))

    heads_per_group = ATT_KV_HEADS // SSM_GROUPS
    for g in range(SSM_GROUPS):
        ssd_group(g)
        for kh in range(g * heads_per_group, (g + 1) * heads_per_group):
            attn_head(kh)

    @pl.when(c == 0)
    def _():
        kn_ref[0, 0:CHUNK, :] = k0_ref[0, CHUNK:2 * CHUNK, :]
        vn_ref[0, 0:CHUNK, :] = v0_ref[0, CHUNK:2 * CHUNK, :]

    @pl.when(c != 0)
    def _():
        kn_ref[0, 0:CHUNK, :] = kn_ref[0, CHUNK:2 * CHUNK, :]
        vn_ref[0, 0:CHUNK, :] = vn_ref[0, CHUNK:2 * CHUNK, :]

    kn_ref[0, CHUNK:2 * CHUNK, :] = proj_ref[:, off_k:off_v]
    vn_ref[0, CHUNK:2 * CHUNK, :] = proj_ref[:, off_v:off_g]

    @pl.when(c == nc - 1)
    def _final_state():
        for t in range(n_tr):
            ssmn_ref[0, 2 * t:2 * t + 2] = ht_s[:, t * LANES:(t + 1) * LANES].T.reshape(
                2, SSM_HEAD_DIM, SSM_STATE)


N_MIXER_INPUTS = 14


def _even_mixer_kernel(first, dims, *refs):
    ins, y_ref, rest = refs[:N_MIXER_INPUTS], refs[N_MIXER_INPUTS], refs[N_MIXER_INPUTS + 1:]

    def write_y(cols, val):
        y_ref[:, cols] = val

    _mixer_chunk(first, dims, pl.program_id(1), pl.num_programs(1), write_y, *ins, *rest)


def _mixer_specs(nproj, conv_dim, n_heads, kvd, inner, row, per_b3, per_b4, const2, const3):
    ins = [
        pl.BlockSpec((CHUNK, nproj), row),
        pl.BlockSpec((CHUNK, LANES), row),
        pl.BlockSpec((1, SSM_CONV - 1, conv_dim), per_b3),
        pl.BlockSpec((1, n_heads, SSM_HEAD_DIM, SSM_STATE), per_b4),
        pl.BlockSpec((1, WINDOW, kvd), per_b3),
        pl.BlockSpec((1, WINDOW, kvd), per_b3),
        pl.BlockSpec((SSM_CONV, conv_dim), const2),
        pl.BlockSpec((1, conv_dim), const2),
        pl.BlockSpec((1, LANES), const2),
        pl.BlockSpec((1, LANES), const2),
        pl.BlockSpec((1, inner), const2),
        pl.BlockSpec((1, inner), const2),
        pl.BlockSpec((ATT_KV_HEADS, 2, LANES), const3),
        pl.BlockSpec((LANES, inner), const2),
    ]
    assert len(ins) == N_MIXER_INPUTS
    state_outs = [
        pl.BlockSpec((1, SSM_CONV - 1, conv_dim), per_b3),
        pl.BlockSpec((1, n_heads, SSM_HEAD_DIM, SSM_STATE), per_b4),
        pl.BlockSpec((1, WINDOW, kvd), per_b3),
        pl.BlockSpec((1, WINDOW, kvd), per_b3),
    ]
    return ins, state_outs


def _mixer_state_shapes(nbatch, conv_dim, n_heads, kvd):
    return [
        jax.ShapeDtypeStruct((nbatch, SSM_CONV - 1, conv_dim), F32),
        jax.ShapeDtypeStruct((nbatch, n_heads, SSM_HEAD_DIM, SSM_STATE), F32),
        jax.ShapeDtypeStruct((nbatch, WINDOW, kvd), F32),
        jax.ShapeDtypeStruct((nbatch, WINDOW, kvd), F32),
    ]


def _mixer_scratch(conv_dim, inner, gn):
    return [
        pltpu.VMEM((CHUNK + 8, conv_dim), F32),
        pltpu.VMEM((CHUNK, inner), F32),
        pltpu.VMEM((CHUNK, 2 * gn), F32),
        pltpu.VMEM((SSM_STATE, inner), F32),
        pltpu.VMEM((ATT_KV_HEADS, 3 * LANES, LANES), BF16),
        pltpu.VMEM((ATT_KV_HEADS, 3 * LANES, 2 * LANES), BF16),
    ]


def _even_mixer_call(first, proj, pdt, conv0, ssm0, k0, v0, convw, convb, dtb, alog, dskip_x, normw,
                     sink_t, expand, nbatch, seq_len):
    nc = seq_len // CHUNK
    nproj = proj.shape[1]
    inner = normw.shape[1]
    conv_dim = convw.shape[1]
    gn = (conv_dim - inner) // 2
    kvd = k0.shape[2]
    n_heads = inner // SSM_HEAD_DIM
    row = lambda b, c: (b * nc + c, 0)
    ins, state_outs = _mixer_specs(nproj, conv_dim, n_heads, kvd, inner, row,
                                   lambda b, c: (b, 0, 0), lambda b, c: (b, 0, 0, 0),
                                   lambda b, c: (0, 0), lambda b, c: (0, 0, 0))
    return pl.pallas_call(
        functools.partial(_even_mixer_kernel, first, (inner, gn, kvd)),
        grid=(nbatch, nc),
        in_specs=ins,
        out_specs=[pl.BlockSpec((CHUNK, 2 * inner), row)] + state_outs,
        out_shape=[jax.ShapeDtypeStruct((nbatch * seq_len, 2 * inner), BF16)]
        + _mixer_state_shapes(nbatch, conv_dim, n_heads, kvd),
        scratch_shapes=_mixer_scratch(conv_dim, inner, gn),
        compiler_params=_cparams(("arbitrary", "arbitrary")),
        name="even_mixer",
    )(proj, pdt, conv0, ssm0, k0, v0, convw, convb, dtb, alog, dskip_x, normw, sink_t, expand)


def _pad_lanes(v):
    return jnp.pad(v, ((0, 0), (0, LANES - v.shape[1])))


def kernel(x_prompt, x_sample, c_prompt, c_sample, cache_k, cache_v, state_conv_a, state_ssm, state_conv_c,
           w_ada, b_ada, norm_pre, norm_post, w_in_even, conv_a_w, conv_a_b, dt_bias, a_log, d_skip, norm_ssm,
           sinks, w_out_even, w_in_odd, conv_c_w, w_out_odd):
    bp, lp, d = x_prompt.shape
    bs, ls, _ = x_sample.shape
    depth = w_ada.shape[0]
    inner = norm_ssm.shape[1]
    n_heads = dt_bias.shape[1]
    conv_dim = conv_a_w.shape[2]
    kvd = cache_k.shape[3] * cache_k.shape[4]
    width = conv_c_w.shape[2]
    assert ls == CHUNK and lp % CHUNK == 0 and inner == d and n_heads <= LANES

    mod = _mod_call(jnp.concatenate([c_prompt, c_sample], axis=0), w_ada, b_ada)

    xp = x_prompt.reshape(bp * lp, d)
    xs = x_sample.reshape(bs * ls, d)
    tm_p = min(TM_IN, lp)
    tm_s = bs * ls
    tm_out_p = min(TM_OUT, lp)
    expand = (jnp.arange(LANES)[:, None] == (jnp.arange(inner) // SSM_HEAD_DIM)[None, :]).astype(BF16)
    zeros_p = dict(
        conv_a=jnp.zeros((bp, SSM_CONV - 1, conv_dim), F32),
        ssm=jnp.zeros((bp, n_heads, SSM_HEAD_DIM, SSM_STATE), F32),
        kv=jnp.zeros((bp, WINDOW, kvd), F32),
        conv_c=jnp.zeros((bp, SHORT_CONV - 1, width), F32),
    )
    outs_p = dict(k=[], v=[], ca=[], ssm=[], cc=[])
    outs_s = dict(k=[], v=[], ca=[], ssm=[], cc=[])
    dt_lo = inner + conv_dim
    dt_hi = dt_lo + n_heads
    tc = TC_ODD
    w_main_all, w_dt_all = _cast_even_call(w_in_even, dt_lo, dt_hi, 128)
    w_in_odd_b = _cast_call(w_in_odd, 128)
    w_out_even_b = _cast_call(w_out_even, 512)
    w_out_odd_b = _cast_call(w_out_odd, 512)

    for layer in range(depth):
        i = layer // 2
        shift, scale, gate = jnp.split(mod[layer], 3, axis=-1)
        rep = lambda v: jnp.repeat(v[:bp], lp // CHUNK, axis=0)
        g_pre = norm_pre[layer][None, :]
        g_post = norm_post[layer][None, :]
        if layer % 2 == 0:
            mixer_params = (
                conv_a_w[i], conv_a_b[i][None, :], _pad_lanes(dt_bias[i][None, :]), _pad_lanes(a_log[i][None, :]),
                jnp.repeat(d_skip[i], SSM_HEAD_DIM)[None, :], norm_ssm[i][None, :],
                jnp.repeat(sinks[i], ATT_HEAD_DIM).reshape(ATT_KV_HEADS, 2, LANES), expand)
            proj, pdt = _inproj_even_call(xp, rep(scale), rep(shift), g_pre, w_main_all, w_dt_all, i, tm_p, TN_IN)
            y, ca, ssm, kn, vn = _even_mixer_call(True, proj, pdt, zeros_p["conv_a"], zeros_p["ssm"],
                                                  zeros_p["kv"], zeros_p["kv"], *mixer_params, bp, lp)
            xp = _outproj_call(y, w_out_even_b, i, xp, rep(gate), g_post, tm_out_p)
            outs_p["k"].append(kn); outs_p["v"].append(vn); outs_p["ca"].append(ca); outs_p["ssm"].append(ssm)
            proj, pdt = _inproj_even_call(xs, scale[bp:], shift[bp:], g_pre, w_main_all, w_dt_all, i, tm_s, TN_IN)
            y, ca, ssm, kn, vn = _even_mixer_call(False, proj, pdt, state_conv_a[i], state_ssm[i],
                                                  cache_k[i].reshape(bs, WINDOW, kvd),
                                                  cache_v[i].reshape(bs, WINDOW, kvd), *mixer_params, bs, ls)
            xs = _outproj_call(y, w_out_even_b, i, xs, gate[bp:], g_post, tm_s)
            outs_s["k"].append(kn); outs_s["v"].append(vn); outs_s["ca"].append(ca); outs_s["ssm"].append(ssm)
        else:
            y, cc = _inproj_odd_call(xp, rep(scale), rep(shift), g_pre, w_in_odd_b, i, conv_c_w[i],
                                     zeros_p["conv_c"], lp, tm_p, tc)
            xp = _outproj_call(y, w_out_odd_b, i, xp, rep(gate), g_post, tm_out_p)
            outs_p["cc"].append(cc)
            y, cc = _inproj_odd_call(xs, scale[bp:], shift[bp:], g_pre, w_in_odd_b, i, conv_c_w[i],
                                     state_conv_c[i], ls, tm_s, tc)
            xs = _outproj_call(y, w_out_odd_b, i, xs, gate[bp:], g_post, tm_s)
            outs_s["cc"].append(cc)

    kv_shape_p = (len(outs_p["k"]), bp, WINDOW) + cache_k.shape[3:]
    kv_shape_s = (len(outs_s["k"]), bs, WINDOW) + cache_k.shape[3:]
    return (xp.reshape(bp, lp, d), xs.reshape(bs, ls, d),
            jnp.stack(outs_p["k"]).reshape(kv_shape_p), jnp.stack(outs_p["v"]).reshape(kv_shape_p),
            jnp.stack(outs_p["ca"]), jnp.stack(outs_p["ssm"]), jnp.stack(outs_p["cc"]),
            jnp.stack(outs_s["k"]).reshape(kv_shape_s), jnp.stack(outs_s["v"]).reshape(kv_shape_s),
            jnp.stack(outs_s["ca"]), jnp.stack(outs_s["ssm"]), jnp.stack(outs_s["cc"]))
```

```python
import functools

import jax
import jax.numpy as jnp
from jax import lax
from jax.experimental import pallas as pl
from jax.experimental.pallas import tpu as pltpu

F32 = jnp.float32
BF16 = jnp.bfloat16

CHUNK = 64
WINDOW = 128
EPS = 1e-6
SSM_HEAD_DIM = 64
SSM_GROUPS = 4
SSM_STATE = 128
SSM_CONV = 4
ATT_HEAD_DIM = 64
ATT_KV_HEADS = 8
SHORT_CONV = 3
LANES = 128

VMEM_LIMIT = 56 * 1024 * 1024
VMEM_LIMIT_EVEN_IN = 60 * 1024 * 1024
TM_IN = 1024
TN_IN = 2048
TM_OUT = 512
TC_ODD = 512
MIXER_CHUNKS_PER_STEP = 2
SUB_ROWS = 256


def _cparams(sem, vmem=VMEM_LIMIT):
    return pltpu.CompilerParams(dimension_semantics=sem, vmem_limit_bytes=vmem)


def _dot(a, b):
    return jnp.dot(a, b, preferred_element_type=F32)


def _dot_nt(a, b):
    return lax.dot_general(a, b, (((1,), (1,)), ((), ())), preferred_element_type=F32)


def _dot_tn(a, b):
    return lax.dot_general(a, b, (((0,), (0,)), ((), ())), preferred_element_type=F32)


def _silu(x):
    return x * jax.nn.sigmoid(x)


def _cast_kernel(x_ref, o_ref):
    o_ref[...] = x_ref[...].astype(BF16)


def _cast_call(w, tr):
    n, r, c = w.shape
    return pl.pallas_call(
        _cast_kernel,
        grid=(n, r // tr),
        in_specs=[pl.BlockSpec((None, tr, c), lambda l, i: (l, i, 0))],
        out_specs=pl.BlockSpec((None, tr, c), lambda l, i: (l, i, 0)),
        out_shape=jax.ShapeDtypeStruct(w.shape, BF16),
        compiler_params=_cparams(("arbitrary", "arbitrary")),
        name="cast_bf16",
    )(w)


def _cast_even_kernel(dt_lo, dt_hi, x_ref, o_ref, odt_ref):
    n_main = o_ref.shape[1]
    o_ref[:, 0:dt_lo] = x_ref[:, 0:dt_lo].astype(BF16)
    o_ref[:, dt_lo:n_main] = x_ref[:, dt_hi:dt_hi + n_main - dt_lo].astype(BF16)
    odt_ref[...] = jnp.zeros(odt_ref.shape, BF16)
    odt_ref[:, 0:dt_hi - dt_lo] = x_ref[:, dt_lo:dt_hi].astype(BF16)


def _cast_even_call(w, dt_lo, dt_hi, tr):
    n, r, c = w.shape
    n_main = c - (dt_hi - dt_lo)
    return pl.pallas_call(
        functools.partial(_cast_even_kernel, dt_lo, dt_hi),
        grid=(n, r // tr),
        in_specs=[pl.BlockSpec((None, tr, c), lambda l, i: (l, i, 0))],
        out_specs=[pl.BlockSpec((None, tr, n_main), lambda l, i: (l, i, 0)),
                   pl.BlockSpec((None, tr, LANES), lambda l, i: (l, i, 0))],
        out_shape=[jax.ShapeDtypeStruct((n, r, n_main), BF16),
                   jax.ShapeDtypeStruct((n, r, LANES), BF16)],
        compiler_params=_cparams(("arbitrary", "arbitrary")),
        name="cast_even",
    )(w)


def _mod_kernel(c_ref, w_ref, b_ref, o_ref):
    sc = _silu(c_ref[...]).astype(BF16)
    o_ref[0] = _dot(sc, w_ref[0].astype(BF16)) + b_ref[0]


def _mod_call(c_all, w_ada, b_ada):
    depth, d, d3 = w_ada.shape
    nb = c_all.shape[0]
    tn = 1024
    return pl.pallas_call(
        _mod_kernel,
        grid=(depth, d3 // tn),
        in_specs=[
            pl.BlockSpec((nb, d), lambda l, j: (0, 0)),
            pl.BlockSpec((1, d, tn), lambda l, j: (l, 0, j)),
            pl.BlockSpec((1, 1, tn), lambda l, j: (l, 0, j)),
        ],
        out_specs=pl.BlockSpec((1, nb, tn), lambda l, j: (l, 0, j)),
        out_shape=jax.ShapeDtypeStruct((depth, nb, d3), F32),
        compiler_params=_cparams(("arbitrary", "arbitrary")),
        name="adaln_mod",
    )(c_all, w_ada, b_ada.reshape(depth, 1, d3))


def _modulated_norm_into(x_ref, scale_ref, shift_ref, g_ref, h_scr, row0, row1):
    g = g_ref[...]
    for cc in range(row0 // CHUNK, row1 // CHUNK):
        rows = slice(cc * CHUNK, (cc + 1) * CHUNK)
        x = x_ref[rows, :]
        ms = jnp.mean(x * x, axis=-1, keepdims=True)
        y = x * lax.rsqrt(ms + EPS) * g
        h = y * (1.0 + scale_ref[cc:cc + 1, :]) + shift_ref[cc:cc + 1, :]
        h_scr[rows, :] = h.astype(BF16)


def _inproj_even_kernel(tm, sub, x_ref, scale_ref, shift_ref, g_ref, w_ref, wdt_ref, o_ref, odt_ref, h_scr):
    j = pl.program_id(1)

    @pl.when(j == 0)
    def _():
        norm = functools.partial(_modulated_norm_into, x_ref, scale_ref, shift_ref, g_ref, h_scr)
        norm(0, sub)
        for sb in range(tm // sub):
            rows = slice(sb * sub, (sb + 1) * sub)
            o_ref[rows, :] = _dot(h_scr[rows, :], w_ref[...])
            odt_ref[rows, :] = _dot(h_scr[rows, :], wdt_ref[...])
            if sb + 1 < tm // sub:
                norm((sb + 1) * sub, (sb + 2) * sub)

    @pl.when(j != 0)
    def _():
        o_ref[...] = _dot(h_scr[...], w_ref[...])


def _inproj_even_call(x, scale_pc, shift_pc, g, w_main, w_dt, li, tm, tn):
    r, d = x.shape
    n = w_main.shape[2]
    nch = tm // CHUNK
    return pl.pallas_call(
        functools.partial(_inproj_even_kernel, tm, min(SUB_ROWS, tm)),
        grid=(r // tm, n // tn),
        in_specs=[
            pl.BlockSpec((tm, d), lambda i, j: (i, 0)),
            pl.BlockSpec((nch, d), lambda i, j: (i, 0)),
            pl.BlockSpec((nch, d), lambda i, j: (i, 0)),
            pl.BlockSpec((1, d), lambda i, j: (0, 0)),
            pl.BlockSpec((None, d, tn), lambda i, j: (li, 0, j)),
            pl.BlockSpec((None, d, LANES), lambda i, j: (li, 0, 0)),
        ],
        out_specs=[
            pl.BlockSpec((tm, tn), lambda i, j: (i, j)),
            pl.BlockSpec((tm, LANES), lambda i, j: (i, 0)),
        ],
        out_shape=[
            jax.ShapeDtypeStruct((r, n), F32),
            jax.ShapeDtypeStruct((r, LANES), F32),
        ],
        scratch_shapes=[pltpu.VMEM((tm, d), BF16)],
        compiler_params=_cparams(("arbitrary", "arbitrary"), VMEM_LIMIT_EVEN_IN),
        name="inproj_even",
    )(x, scale_pc, shift_pc, g, w_main, w_dt)


def _inproj_odd_kernel(tm, sub, tc, tiles_per_seq, per_chunk_start,
                       x_ref, scale_ref, shift_ref, g_ref, wu_ref, wb_ref, wc_ref, wg_ref, cw_ref, c0_ref,
                       y_ref, cn_ref, h_scr, ebuf, halo):
    i = pl.program_id(0)
    j = pl.program_id(1)

    norm = functools.partial(_modulated_norm_into, x_ref, scale_ref, shift_ref, g_ref, h_scr)
    w0 = cw_ref[0:1, :]
    w1 = cw_ref[1:2, :]
    w2 = cw_ref[2:3, :]
    if not per_chunk_start:
        @pl.when(i % tiles_per_seq == 0)
        def _():
            ebuf[CHUNK + 6:CHUNK + 8, :] = c0_ref[0]

        @pl.when(i % tiles_per_seq != 0)
        def _():
            ebuf[CHUNK + 6:CHUNK + 8, :] = halo[j, 6:8, :]

    def body(with_norm):
        if with_norm:
            norm(0, sub)
        for sb in range(tm // sub):
            hs = h_scr[sb * sub:(sb + 1) * sub, :]
            pu = _dot(hs, wu_ref[...])
            pb = _dot(hs, wb_ref[...])
            pc = _dot(hs, wc_ref[...])
            pg = _dot(hs, wg_ref[...])
            if with_norm and sb + 1 < tm // sub:
                norm((sb + 1) * sub, (sb + 2) * sub)
            for c2 in range(sub // CHUNK):
                cc = sb * (sub // CHUNK) + c2
                r2 = slice(c2 * CHUNK, (c2 + 1) * CHUNK)
                rows = slice(cc * CHUNK, (cc + 1) * CHUNK)
                if per_chunk_start:
                    ebuf[6:8, :] = c0_ref[cc]
                else:
                    ebuf[6:8, :] = ebuf[CHUNK + 6:CHUNK + 8, :]
                ebuf[8:8 + CHUNK, :] = pc[r2] * pu[r2]
                conv = w0 * ebuf[6:6 + CHUNK, :] + w1 * ebuf[7:7 + CHUNK, :] + w2 * ebuf[8:8 + CHUNK, :]
                y_ref[rows, :] = (pb[r2] * conv * _silu(pg[r2])).astype(BF16)
                if per_chunk_start:
                    cn_ref[cc] = ebuf[CHUNK + 6:CHUNK + 8, :]

    pl.when(j == 0)(lambda: body(True))
    pl.when(j != 0)(lambda: body(False))
    if not per_chunk_start:
        tail = ebuf[CHUNK + 6:CHUNK + 8, :]
        halo[j, 6:8, :] = tail
        cn_ref[0] = tail


def _inproj_odd_call(x, scale_pc, shift_pc, g, w_all, li, conv_w, conv0, seq_len, tm, tc):
    r, d = x.shape
    width = conv_w.shape[1]
    nct = width // tc
    nch = tm // CHUNK
    sub = min(SUB_ROWS, tm)
    per_chunk_start = seq_len == CHUNK
    if per_chunk_start:
        tiles_per_seq = 1
        nb = nch
        c0_map = lambda i, j: (i, 0, j)
    else:
        assert seq_len % tm == 0
        tiles_per_seq = seq_len // tm
        nb = 1
        c0_map = lambda i, j: (i // tiles_per_seq, 0, j)
    wspec = lambda q: pl.BlockSpec((None, d, tc), lambda i, j: (li, 0, q * nct + j))
    y, tails = pl.pallas_call(
        functools.partial(_inproj_odd_kernel, tm, sub, tc, tiles_per_seq, per_chunk_start),
        grid=(r // tm, nct),
        in_specs=[
            pl.BlockSpec((tm, d), lambda i, j: (i, 0)),
            pl.BlockSpec((nch, d), lambda i, j: (i, 0)),
            pl.BlockSpec((nch, d), lambda i, j: (i, 0)),
            pl.BlockSpec((1, d), lambda i, j: (0, 0)),
            wspec(0), wspec(1), wspec(2), wspec(3),
            pl.BlockSpec((SHORT_CONV, tc), lambda i, j: (0, j)),
            pl.BlockSpec((nb, SHORT_CONV - 1, tc), c0_map),
        ],
        out_specs=[
            pl.BlockSpec((tm, tc), lambda i, j: (i, j)),
            pl.BlockSpec((nb, SHORT_CONV - 1, tc), lambda i, j: (i, 0, j)),
        ],
        out_shape=[
            jax.ShapeDtypeStruct((r, width), BF16),
            jax.ShapeDtypeStruct((r // tm * nb, SHORT_CONV - 1, width), F32),
        ],
        scratch_shapes=[
            pltpu.VMEM((tm, d), BF16),
            pltpu.VMEM((CHUNK + 8, tc), F32),
            pltpu.VMEM((nct, 8, tc), F32),
        ],
        compiler_params=_cparams(("arbitrary", "arbitrary")),
        name="inproj_odd",
    )(x, scale_pc, shift_pc, g, w_all, w_all, w_all, w_all, conv_w, conv0)
    return y, tails[tiles_per_seq - 1::tiles_per_seq]


def _outproj_kernel(tm, sub, y_ref, w_ref, x_ref, gate_ref, np_ref, o_ref):
    npw = np_ref[...]
    for sb in range(tm // sub):
        o = _dot(y_ref[sb * sub:(sb + 1) * sub, :], w_ref[...])
        for c2 in range(sub // CHUNK):
            cc = sb * (sub // CHUNK) + c2
            rows = slice(cc * CHUNK, (cc + 1) * CHUNK)
            oc = o[c2 * CHUNK:(c2 + 1) * CHUNK, :]
            ms = jnp.mean(oc * oc, axis=-1, keepdims=True)
            r = oc * lax.rsqrt(ms + EPS) * npw
            o_ref[rows, :] = x_ref[rows, :] + gate_ref[cc:cc + 1, :] * r


def _outproj_call(y, w_all, li, x, gate_pc, npost, tm):
    r, k = y.shape
    d = x.shape[1]
    nch = tm // CHUNK
    sub = min(SUB_ROWS, tm)
    return pl.pallas_call(
        functools.partial(_outproj_kernel, tm, sub),
        grid=(r // tm,),
        in_specs=[
            pl.BlockSpec((tm, k), lambda i: (i, 0)),
            pl.BlockSpec((None, k, d), lambda i: (li, 0, 0), pipeline_mode=pl.Buffered(1)),
            pl.BlockSpec((tm, d), lambda i: (i, 0)),
            pl.BlockSpec((nch, d), lambda i: (i, 0)),
            pl.BlockSpec((1, d), lambda i: (0, 0)),
        ],
        out_specs=pl.BlockSpec((tm, d), lambda i: (i, 0)),
        out_shape=jax.ShapeDtypeStruct((r, d), F32),
        compiler_params=_cparams(("arbitrary",)),
        name="outproj",
    )(y, w_all, x, gate_pc, npost)


def _write_kv_slot(kv, buf, base):
    lane = lax.broadcasted_iota(jnp.int32, (CHUNK, LANES), 1)
    lo = lane < ATT_HEAD_DIM
    for m in range(kv.shape[1] // LANES):
        blk = kv[:, m * LANES:(m + 1) * LANES]
        sw = pltpu.roll(blk, ATT_HEAD_DIM, 1)
        pieces = (
            (2 * m, jnp.where(lo, blk, 0.0), jnp.where(lo, 0.0, sw)),
            (2 * m + 1, jnp.where(lo, sw, 0.0), jnp.where(lo, 0.0, blk)),
        )
        for head, low, high in pieces:
            buf[head, pl.ds(base, CHUNK), 0:LANES] = low.astype(BF16)
            buf[head, pl.ds(base + CHUNK, CHUNK), 0:LANES] = high.astype(BF16)


def _mixer_chunk(first, dims, c, nc, maybe_first, maybe_last, write_y,
                 proj_ref, pdt_ref, conv0_ref, ssm0_ref, k0_ref, v0_ref,
                 convw_ref, convb_ref, dtb_ref, alog_ref, dskip_ref, normw_ref, sink_ref, e_ref,
                 convn_ref, ssmn_ref, kn_ref, vn_ref,
                 xext, xs_s, bc_s, ht_s, kbd, vbd):
    inner, gn, kvd = dims
    gw = inner // SSM_GROUPS
    off_xbc = inner
    off_q = off_xbc + inner + 2 * gn
    off_k = off_q + inner
    off_v = off_k + kvd
    off_g = off_v + kvd
    n_tr = inner // LANES

    lane64 = lax.broadcasted_iota(jnp.int32, (CHUNK, LANES), 1) < ATT_HEAD_DIM
    lane128 = lax.broadcasted_iota(jnp.int32, (2 * CHUNK, LANES), 1) < ATT_HEAD_DIM

    def _init():
        xext[5:8, :] = conv0_ref[0]
        for t in range(n_tr):
            ht_s[:, t * LANES:(t + 1) * LANES] = ssm0_ref[0, 2 * t:2 * t + 2].reshape(LANES, SSM_STATE).T
        for s in range(WINDOW // CHUNK):
            _write_kv_slot(k0_ref[0, s * CHUNK:(s + 1) * CHUNK, :], kbd, s * LANES)
            _write_kv_slot(v0_ref[0, s * CHUNK:(s + 1) * CHUNK, :], vbd, s * LANES)
        rowi = lax.broadcasted_iota(jnp.int32, (3 * LANES, LANES), 0)
        lanei = lax.broadcasted_iota(jnp.int32, (3 * LANES, LANES), 1)
        ones_bd = jnp.where((rowi % LANES < CHUNK) == (lanei < ATT_HEAD_DIM), 1.0, 0.0).astype(BF16)
        for h in range(ATT_KV_HEADS):
            vbd[h, :, LANES:2 * LANES] = ones_bd

    if maybe_first:
        pl.when(c == 0)(_init)

    xext[8:8 + CHUNK, :] = proj_ref[:, off_xbc:off_q]
    piece = 512
    for t in range((inner + 2 * gn) // piece):
        cs = slice(t * piece, (t + 1) * piece)
        acc = convb_ref[:, cs] + convw_ref[3:4, cs] * xext[8:8 + CHUNK, cs]
        for i in range(SSM_CONV - 1):
            acc = acc + convw_ref[i:i + 1, cs] * xext[5 + i:5 + i + CHUNK, cs]
        act = _silu(acc)
        if t * piece < inner:
            xs_s[:, cs] = act
        else:
            bc_s[:, t * piece - inner:(t + 1) * piece - inner] = act
    tail = xext[CHUNK + 5:CHUNK + 8, :]
    convn_ref[0] = tail
    xext[5:8, :] = tail

    dt = jax.nn.softplus(pdt_ref[...] + dtb_ref[...])
    da = dt * (-jnp.exp(alog_ref[...]))
    ri = lax.broadcasted_iota(jnp.int32, (CHUNK, CHUNK), 0)
    ci = lax.broadcasted_iota(jnp.int32, (CHUNK, CHUNK), 1)
    causal = ri >= ci
    tri = jnp.where(causal, 1.0, 0.0).astype(BF16)
    d_hi = da.astype(BF16)
    r1 = da - d_hi.astype(F32)
    d_mid = r1.astype(BF16)
    d_lo = (r1 - d_mid.astype(F32)).astype(BF16)
    a_cum = _dot(tri, d_hi) + _dot(tri, d_mid) + _dot(tri, d_lo)
    a_last = a_cum[CHUNK - 1:CHUNK, :]
    stack = jnp.concatenate([jnp.exp(a_cum), dt, jnp.exp(a_last - a_cum)], axis=0)
    s_hi = stack.astype(BF16)
    s_lo = (stack - s_hi.astype(F32)).astype(BF16)
    ex = _dot(s_hi, e_ref[...]) + _dot(s_lo, e_ref[...])
    ea = ex[0:CHUNK]
    xs = xs_s[...]
    xdt = xs * ex[CHUNK:2 * CHUNK]
    xdt_b = xdt.astype(BF16)
    xsc_b = (xdt * ex[2 * CHUNK:3 * CHUNK]).astype(BF16)
    act_t = jnp.concatenate([a_cum, jnp.zeros((CHUNK, LANES), F32)], axis=0).T

    pairs_per_group = gw // LANES

    def ssd_group(g):
        gcs = slice(g * gw, (g + 1) * gw)
        bg = bc_s[:, g * SSM_STATE:(g + 1) * SSM_STATE].astype(BF16)
        cg = bc_s[:, gn + g * SSM_STATE:gn + (g + 1) * SSM_STATE].astype(BF16)
        cb = _dot_nt(cg, bg)
        h_prev = ht_s[:, gcs]
        y_off = _dot(cg, h_prev.astype(BF16)) * ea[:, gcs]
        ht_s[:, gcs] = h_prev * ea[CHUNK - 1:CHUNK, gcs] + _dot_tn(bg, xsc_b[:, gcs])
        y_pairs = []
        for pm in range(pairs_per_group):
            m = g * pairs_per_group + pm
            ps = slice(m * LANES, (m + 1) * LANES)
            halves = []
            for e in range(2):
                h = 2 * m + e
                seg = a_cum[:, h:h + 1] - act_t[h:h + 1, 0:CHUNK]
                dec = jnp.exp(jnp.where(causal, seg, -jnp.inf))
                halves.append(_dot((cb * dec).astype(BF16), xdt_b[:, ps]))
            yd = jnp.where(lane64, halves[0], halves[1])
            y_pairs.append(yd + y_off[:, pm * LANES:(pm + 1) * LANES] + dskip_ref[:, ps] * xs[:, ps])
        yg = jnp.concatenate(y_pairs, axis=1) * _silu(proj_ref[:, gcs])
        yg = yg * lax.rsqrt(jnp.mean(yg * yg, axis=-1, keepdims=True) + EPS)
        write_y(gcs, (yg * normw_ref[:, gcs]).astype(BF16))

    k_new = proj_ref[:, off_k:off_v]
    v_new = proj_ref[:, off_v:off_g]
    base = pl.multiple_of(lax.rem(c + 2, 3) * LANES, LANES)
    _write_kv_slot(k_new, kbd, base)
    _write_kv_slot(v_new, vbd, base)

    neg_inf = jnp.float32(-jnp.inf)
    qscale = ATT_HEAD_DIM ** -0.5

    def attn_head(kh):
        qa = slice(off_q + 2 * kh * LANES, off_q + (2 * kh + 1) * LANES)
        qb = slice(off_q + (2 * kh + 1) * LANES, off_q + (2 * kh + 2) * LANES)
        qs = (jnp.concatenate([proj_ref[:, qa], proj_ref[:, qb]], axis=0) * qscale).astype(BF16)
        s = _dot_nt(qs, kbd[kh])
        sb = [s[:, t * LANES:(t + 1) * LANES] for t in range(3)]
        if first:
            sb[0] = jnp.where(c == 0, neg_inf, sb[0])
            sb[1] = jnp.where(c <= 1, neg_inf, sb[1])
        mb = jnp.maximum(jnp.maximum(sb[0], sb[1]), sb[2])
        m_lo = jnp.max(jnp.where(lane128, mb, neg_inf), axis=-1, keepdims=True)
        m_hi = jnp.max(jnp.where(lane128, neg_inf, mb), axis=-1, keepdims=True)
        sink = jnp.concatenate([jnp.broadcast_to(sink_ref[kh, 0:1, :], (CHUNK, LANES)),
                                jnp.broadcast_to(sink_ref[kh, 1:2, :], (CHUNK, LANES))], axis=0)
        mx = jnp.maximum(jnp.where(lane128, m_lo, m_hi), sink)
        p = jnp.concatenate([jnp.exp(x - mx) for x in sb], axis=1).astype(BF16)
        ov = _dot(p, vbd[kh])
        denom = ov[:, LANES:2 * LANES] + jnp.exp(sink - mx)
        o = ov[:, 0:LANES] / denom
        ga = slice(off_g + 2 * kh * LANES, off_g + (2 * kh + 1) * LANES)
        gb = slice(off_g + (2 * kh + 1) * LANES, off_g + (2 * kh + 2) * LANES)
        ya = slice(inner + 2 * kh * LANES, inner + (2 * kh + 1) * LANES)
        yb = slice(inner + (2 * kh + 1) * LANES, inner + (2 * kh + 2) * LANES)
        write_y(ya, (o[0:CHUNK] * _silu(proj_ref[:, ga])).astype(BF16))
        write_y(yb, (o[CHUNK:2 * CHUNK] * _silu(proj_ref[:, gb])).astype(BF16))

    heads_per_group = ATT_KV_HEADS // SSM_GROUPS
    for g in range(SSM_GROUPS):
        ssd_group(g)
        for kh in range(g * heads_per_group, (g + 1) * heads_per_group):
            attn_head(kh)

    def _window_from_state():
        kn_ref[0, 0:CHUNK, :] = k0_ref[0, CHUNK:2 * CHUNK, :]
        vn_ref[0, 0:CHUNK, :] = v0_ref[0, CHUNK:2 * CHUNK, :]

    def _window_shift():
        kn_ref[0, 0:CHUNK, :] = kn_ref[0, CHUNK:2 * CHUNK, :]
        vn_ref[0, 0:CHUNK, :] = vn_ref[0, CHUNK:2 * CHUNK, :]

    if maybe_first:
        pl.when(c == 0)(_window_from_state)
        pl.when(c != 0)(_window_shift)
    else:
        _window_shift()
    kn_ref[0, CHUNK:2 * CHUNK, :] = proj_ref[:, off_k:off_v]
    vn_ref[0, CHUNK:2 * CHUNK, :] = proj_ref[:, off_v:off_g]

    def _final_state():
        for t in range(n_tr):
            ssmn_ref[0, 2 * t:2 * t + 2] = ht_s[:, t * LANES:(t + 1) * LANES].T.reshape(
                2, SSM_HEAD_DIM, SSM_STATE)

    if maybe_last:
        pl.when(c == nc - 1)(_final_state)


N_MIXER_INPUTS = 14


def _even_mixer_kernel(first, dims, cps, *refs):
    ins, y_ref, rest = refs[:N_MIXER_INPUTS], refs[N_MIXER_INPUTS], refs[N_MIXER_INPUTS + 1:]
    proj_ref, pdt_ref = ins[:2]
    for u in range(cps):
        rows = pl.ds(u * CHUNK, CHUNK)

        def write_y(cols, val, rows=rows):
            y_ref[rows, cols] = val

        _mixer_chunk(first, dims, pl.program_id(1) * cps + u, pl.num_programs(1) * cps,
                     u == 0, u == cps - 1, write_y,
                     proj_ref.at[rows], pdt_ref.at[rows], *ins[2:], *rest)


def _mixer_specs(rows, nproj, conv_dim, n_heads, kvd, inner, row, per_b3, per_b4, const2, const3):
    ins = [
        pl.BlockSpec((rows, nproj), row),
        pl.BlockSpec((rows, LANES), row),
        pl.BlockSpec((1, SSM_CONV - 1, conv_dim), per_b3),
        pl.BlockSpec((1, n_heads, SSM_HEAD_DIM, SSM_STATE), per_b4),
        pl.BlockSpec((1, WINDOW, kvd), per_b3),
        pl.BlockSpec((1, WINDOW, kvd), per_b3),
        pl.BlockSpec((SSM_CONV, conv_dim), const2),
        pl.BlockSpec((1, conv_dim), const2),
        pl.BlockSpec((1, LANES), const2),
        pl.BlockSpec((1, LANES), const2),
        pl.BlockSpec((1, inner), const2),
        pl.BlockSpec((1, inner), const2),
        pl.BlockSpec((ATT_KV_HEADS, 2, LANES), const3),
        pl.BlockSpec((LANES, inner), const2),
    ]
    assert len(ins) == N_MIXER_INPUTS
    state_outs = [
        pl.BlockSpec((1, SSM_CONV - 1, conv_dim), per_b3),
        pl.BlockSpec((1, n_heads, SSM_HEAD_DIM, SSM_STATE), per_b4),
        pl.BlockSpec((1, WINDOW, kvd), per_b3),
        pl.BlockSpec((1, WINDOW, kvd), per_b3),
    ]
    return ins, state_outs


def _mixer_state_shapes(nbatch, conv_dim, n_heads, kvd):
    return [
        jax.ShapeDtypeStruct((nbatch, SSM_CONV - 1, conv_dim), F32),
        jax.ShapeDtypeStruct((nbatch, n_heads, SSM_HEAD_DIM, SSM_STATE), F32),
        jax.ShapeDtypeStruct((nbatch, WINDOW, kvd), F32),
        jax.ShapeDtypeStruct((nbatch, WINDOW, kvd), F32),
    ]


def _mixer_scratch(conv_dim, inner, gn):
    return [
        pltpu.VMEM((CHUNK + 8, conv_dim), F32),
        pltpu.VMEM((CHUNK, inner), F32),
        pltpu.VMEM((CHUNK, 2 * gn), F32),
        pltpu.VMEM((SSM_STATE, inner), F32),
        pltpu.VMEM((ATT_KV_HEADS, 3 * LANES, LANES), BF16),
        pltpu.VMEM((ATT_KV_HEADS, 3 * LANES, 2 * LANES), BF16),
    ]


def _even_mixer_call(first, proj, pdt, conv0, ssm0, k0, v0, convw, convb, dtb, alog, dskip_x, normw,
                     sink_t, expand, nbatch, seq_len):
    nc = seq_len // CHUNK
    nproj = proj.shape[1]
    inner = normw.shape[1]
    conv_dim = convw.shape[1]
    gn = (conv_dim - inner) // 2
    kvd = k0.shape[2]
    n_heads = inner // SSM_HEAD_DIM
    cps = MIXER_CHUNKS_PER_STEP if nc % MIXER_CHUNKS_PER_STEP == 0 else 1
    nsteps = nc // cps
    row = lambda b, s: (b * nsteps + s, 0)
    ins, state_outs = _mixer_specs(cps * CHUNK, nproj, conv_dim, n_heads, kvd, inner, row,
                                   lambda b, s: (b, 0, 0), lambda b, s: (b, 0, 0, 0),
                                   lambda b, s: (0, 0), lambda b, s: (0, 0, 0))
    return pl.pallas_call(
        functools.partial(_even_mixer_kernel, first, (inner, gn, kvd), cps),
        grid=(nbatch, nsteps),
        in_specs=ins,
        out_specs=[pl.BlockSpec((cps * CHUNK, 2 * inner), row)] + state_outs,
        out_shape=[jax.ShapeDtypeStruct((nbatch * seq_len, 2 * inner), BF16)]
        + _mixer_state_shapes(nbatch, conv_dim, n_heads, kvd),
        scratch_shapes=_mixer_scratch(conv_dim, inner, gn),
        compiler_params=_cparams(("arbitrary", "arbitrary")),
        name="even_mixer",
    )(proj, pdt, conv0, ssm0, k0, v0, convw, convb, dtb, alog, dskip_x, normw, sink_t, expand)


def _pad_lanes(v):
    return jnp.pad(v, ((0, 0), (0, LANES - v.shape[1])))


def kernel(x_prompt, x_sample, c_prompt, c_sample, cache_k, cache_v, state_conv_a, state_ssm, state_conv_c,
           w_ada, b_ada, norm_pre, norm_post, w_in_even, conv_a_w, conv_a_b, dt_bias, a_log, d_skip, norm_ssm,
           sinks, w_out_even, w_in_odd, conv_c_w, w_out_odd):
    bp, lp, d = x_prompt.shape
    bs, ls, _ = x_sample.shape
    depth = w_ada.shape[0]
    inner = norm_ssm.shape[1]
    n_heads = dt_bias.shape[1]
    conv_dim = conv_a_w.shape[2]
    kvd = cache_k.shape[3] * cache_k.shape[4]
    width = conv_c_w.shape[2]
    assert ls == CHUNK and lp % CHUNK == 0 and inner == d and n_heads <= LANES

    mod = _mod_call(jnp.concatenate([c_prompt, c_sample], axis=0), w_ada, b_ada)

    xp = x_prompt.reshape(bp * lp, d)
    xs = x_sample.reshape(bs * ls, d)
    tm_p = min(TM_IN, lp)
    tm_s = bs * ls
    tm_out_p = min(TM_OUT, lp)
    expand = (jnp.arange(LANES)[:, None] == (jnp.arange(inner) // SSM_HEAD_DIM)[None, :]).astype(BF16)
    zeros_p = dict(
        conv_a=jnp.zeros((bp, SSM_CONV - 1, conv_dim), F32),
        ssm=jnp.zeros((bp, n_heads, SSM_HEAD_DIM, SSM_STATE), F32),
        kv=jnp.zeros((bp, WINDOW, kvd), F32),
        conv_c=jnp.zeros((bp, SHORT_CONV - 1, width), F32),
    )
    outs_p = dict(k=[], v=[], ca=[], ssm=[], cc=[])
    outs_s = dict(k=[], v=[], ca=[], ssm=[], cc=[])
    dt_lo = inner + conv_dim
    dt_hi = dt_lo + n_heads
    tc = TC_ODD
    w_main_all, w_dt_all = _cast_even_call(w_in_even, dt_lo, dt_hi, 128)
    w_in_odd_b = _cast_call(w_in_odd, 128)
    w_out_even_b = _cast_call(w_out_even, 512)
    w_out_odd_b = _cast_call(w_out_odd, 512)

    for layer in range(depth):
        i = layer // 2
        shift, scale, gate = jnp.split(mod[layer], 3, axis=-1)
        rep = lambda v: jnp.repeat(v[:bp], lp // CHUNK, axis=0)
        g_pre = norm_pre[layer][None, :]
        g_post = norm_post[layer][None, :]
        if layer % 2 == 0:
            mixer_params = (
                conv_a_w[i], conv_a_b[i][None, :], _pad_lanes(dt_bias[i][None, :]), _pad_lanes(a_log[i][None, :]),
                jnp.repeat(d_skip[i], SSM_HEAD_DIM)[None, :], norm_ssm[i][None, :],
                jnp.repeat(sinks[i], ATT_HEAD_DIM).reshape(ATT_KV_HEADS, 2, LANES), expand)
            proj, pdt = _inproj_even_call(xp, rep(scale), rep(shift), g_pre, w_main_all, w_dt_all, i, tm_p, TN_IN)
            y, ca, ssm, kn, vn = _even_mixer_call(True, proj, pdt, zeros_p["conv_a"], zeros_p["ssm"],
                                                  zeros_p["kv"], zeros_p["kv"], *mixer_params, bp, lp)
            xp = _outproj_call(y, w_out_even_b, i, xp, rep(gate), g_post, tm_out_p)
            outs_p["k"].append(kn); outs_p["v"].append(vn); outs_p["ca"].append(ca); outs_p["ssm"].append(ssm)
            proj, pdt = _inproj_even_call(xs, scale[bp:], shift[bp:], g_pre, w_main_all, w_dt_all, i, tm_s, TN_IN)
            y, ca, ssm, kn, vn = _even_mixer_call(False, proj, pdt, state_conv_a[i], state_ssm[i],
                                                  cache_k[i].reshape(bs, WINDOW, kvd),
                                                  cache_v[i].reshape(bs, WINDOW, kvd), *mixer_params, bs, ls)
            xs = _outproj_call(y, w_out_even_b, i, xs, gate[bp:], g_post, tm_s)
            outs_s["k"].append(kn); outs_s["v"].append(vn); outs_s["ca"].append(ca); outs_s["ssm"].append(ssm)
        else:
            y, cc = _inproj_odd_call(xp, rep(scale), rep(shift), g_pre, w_in_odd_b, i, conv_c_w[i],
                                     zeros_p["conv_c"], lp, tm_p, tc)
            xp = _outproj_call(y, w_out_odd_b, i, xp, rep(gate), g_post, tm_out_p)
            outs_p["cc"].append(cc)
            y, cc = _inproj_odd_call(xs, scale[bp:], shift[bp:], g_pre, w_in_odd_b, i, conv_c_w[i],
                                     state_conv_c[i], ls, tm_s, tc)
            xs = _outproj_call(y, w_out_odd_b, i, xs, gate[bp:], g_post, tm_s)
            outs_s["cc"].append(cc)

    kv_shape_p = (len(outs_p["k"]), bp, WINDOW) + cache_k.shape[3:]
    kv_shape_s = (len(outs_s["k"]), bs, WINDOW) + cache_k.shape[3:]
    return (xp.reshape(bp, lp, d), xs.reshape(bs, ls, d),
            jnp.stack(outs_p["k"]).reshape(kv_shape_p), jnp.stack(outs_p["v"]).reshape(kv_shape_p),
            jnp.stack(outs_p["ca"]), jnp.stack(outs_p["ssm"]), jnp.stack(outs_p["cc"]),
            jnp.stack(outs_s["k"]).reshape(kv_shape_s), jnp.stack(outs_s["v"]).reshape(kv_shape_s),
            jnp.stack(outs_s["ca"]), jnp.stack(outs_s["ssm"]), jnp.stack(outs_s["cc"]))
```

```python
import functools

import jax
import jax.numpy as jnp
from jax import lax
from jax.experimental import pallas as pl
from jax.experimental.pallas import tpu as pltpu

F32 = jnp.float32
BF16 = jnp.bfloat16

CHUNK = 64
WINDOW = 128
EPS = 1e-6
SSM_HEAD_DIM = 64
SSM_GROUPS = 4
SSM_STATE = 128
SSM_CONV = 4
ATT_HEAD_DIM = 64
ATT_KV_HEADS = 8
SHORT_CONV = 3
LANES = 128

VMEM_LIMIT = 56 * 1024 * 1024
TM_IN = 1024
TN_IN = 2048
TM_OUT = 512
TC_ODD = 512
MIXER_CHUNKS_PER_STEP = 2
ATT_HEADS_AFTER_CONV_PIECE = {}
ATT_HEADS_AFTER_GROUP = ((0, 1), (2, 3), (4, 5), (6, 7))
SUB_ROWS = 256


def _cparams(sem, vmem=VMEM_LIMIT):
    return pltpu.CompilerParams(dimension_semantics=sem, vmem_limit_bytes=vmem)


def _dot(a, b):
    return jnp.dot(a, b, preferred_element_type=F32)


def _dot_nt(a, b):
    return lax.dot_general(a, b, (((1,), (1,)), ((), ())), preferred_element_type=F32)


def _dot_tn(a, b):
    return lax.dot_general(a, b, (((0,), (0,)), ((), ())), preferred_element_type=F32)


def _silu(x):
    return x * jax.nn.sigmoid(x)


def _cast_kernel(x_ref, o_ref):
    o_ref[...] = x_ref[...].astype(BF16)


def _cast_call(w, tr):
    n, r, c = w.shape
    return pl.pallas_call(
        _cast_kernel,
        grid=(n, r // tr),
        in_specs=[pl.BlockSpec((None, tr, c), lambda l, i: (l, i, 0))],
        out_specs=pl.BlockSpec((None, tr, c), lambda l, i: (l, i, 0)),
        out_shape=jax.ShapeDtypeStruct(w.shape, BF16),
        compiler_params=_cparams(("arbitrary", "arbitrary")),
        name="cast_bf16",
    )(w)


def _cast_even_kernel(dt_lo, dt_hi, x_ref, o_ref, odt_ref):
    n_main = o_ref.shape[1]
    o_ref[:, 0:dt_lo] = x_ref[:, 0:dt_lo].astype(BF16)
    o_ref[:, dt_lo:n_main] = x_ref[:, dt_hi:dt_hi + n_main - dt_lo].astype(BF16)
    odt_ref[...] = jnp.zeros(odt_ref.shape, BF16)
    odt_ref[:, 0:dt_hi - dt_lo] = x_ref[:, dt_lo:dt_hi].astype(BF16)


def _cast_even_call(w, dt_lo, dt_hi, tr):
    n, r, c = w.shape
    n_main = c - (dt_hi - dt_lo)
    return pl.pallas_call(
        functools.partial(_cast_even_kernel, dt_lo, dt_hi),
        grid=(n, r // tr),
        in_specs=[pl.BlockSpec((None, tr, c), lambda l, i: (l, i, 0))],
        out_specs=[pl.BlockSpec((None, tr, n_main), lambda l, i: (l, i, 0)),
                   pl.BlockSpec((None, tr, LANES), lambda l, i: (l, i, 0))],
        out_shape=[jax.ShapeDtypeStruct((n, r, n_main), BF16),
                   jax.ShapeDtypeStruct((n, r, LANES), BF16)],
        compiler_params=_cparams(("arbitrary", "arbitrary")),
        name="cast_even",
    )(w)


def _mod_kernel(c_ref, w_ref, b_ref, o_ref):
    sc = _silu(c_ref[...]).astype(BF16)
    o_ref[0] = _dot(sc, w_ref[0].astype(BF16)) + b_ref[0]


def _mod_call(c_all, w_ada, b_ada):
    depth, d, d3 = w_ada.shape
    nb = c_all.shape[0]
    tn = 1024
    return pl.pallas_call(
        _mod_kernel,
        grid=(depth, d3 // tn),
        in_specs=[
            pl.BlockSpec((nb, d), lambda l, j: (0, 0)),
            pl.BlockSpec((1, d, tn), lambda l, j: (l, 0, j)),
            pl.BlockSpec((1, 1, tn), lambda l, j: (l, 0, j)),
        ],
        out_specs=pl.BlockSpec((1, nb, tn), lambda l, j: (l, 0, j)),
        out_shape=jax.ShapeDtypeStruct((depth, nb, d3), F32),
        compiler_params=_cparams(("arbitrary", "arbitrary")),
        name="adaln_mod",
    )(c_all, w_ada, b_ada.reshape(depth, 1, d3))


def _modulated_norm_into(x_ref, scale_ref, shift_ref, g_ref, h_scr, row0, row1):
    g = g_ref[...]
    for cc in range(row0 // CHUNK, row1 // CHUNK):
        rows = slice(cc * CHUNK, (cc + 1) * CHUNK)
        x = x_ref[rows, :]
        ms = jnp.mean(x * x, axis=-1, keepdims=True)
        y = x * lax.rsqrt(ms + EPS) * g
        h = y * (1.0 + scale_ref[cc:cc + 1, :]) + shift_ref[cc:cc + 1, :]
        h_scr[rows, :] = h.astype(BF16)


def _inproj_even_kernel(tm, sub, x_ref, scale_ref, shift_ref, g_ref, w_ref, wdt_ref, o_ref, odt_ref, h_scr):
    j = pl.program_id(1)

    @pl.when(j == 0)
    def _():
        norm = functools.partial(_modulated_norm_into, x_ref, scale_ref, shift_ref, g_ref, h_scr)
        norm(0, sub)
        for sb in range(tm // sub):
            rows = slice(sb * sub, (sb + 1) * sub)
            o_ref[rows, :] = _dot(h_scr[rows, :], w_ref[...]).astype(o_ref.dtype)
            odt_ref[rows, :] = _dot(h_scr[rows, :], wdt_ref[...])
            if sb + 1 < tm // sub:
                norm((sb + 1) * sub, (sb + 2) * sub)

    @pl.when(j != 0)
    def _():
        o_ref[...] = _dot(h_scr[...], w_ref[...]).astype(o_ref.dtype)


def _inproj_even_call(x, scale_pc, shift_pc, g, w_main, w_dt, li, tm, tn):
    r, d = x.shape
    n = w_main.shape[2]
    nch = tm // CHUNK
    return pl.pallas_call(
        functools.partial(_inproj_even_kernel, tm, min(SUB_ROWS, tm)),
        grid=(r // tm, n // tn),
        in_specs=[
            pl.BlockSpec((tm, d), lambda i, j: (i, 0)),
            pl.BlockSpec((nch, d), lambda i, j: (i, 0)),
            pl.BlockSpec((nch, d), lambda i, j: (i, 0)),
            pl.BlockSpec((1, d), lambda i, j: (0, 0)),
            pl.BlockSpec((None, d, tn), lambda i, j: (li, 0, j)),
            pl.BlockSpec((None, d, LANES), lambda i, j: (li, 0, 0)),
        ],
        out_specs=[
            pl.BlockSpec((tm, tn), lambda i, j: (i, j)),
            pl.BlockSpec((tm, LANES), lambda i, j: (i, 0)),
        ],
        out_shape=[
            jax.ShapeDtypeStruct((r, n), BF16),
            jax.ShapeDtypeStruct((r, LANES), F32),
        ],
        scratch_shapes=[pltpu.VMEM((tm, d), BF16)],
        compiler_params=_cparams(("arbitrary", "arbitrary")),
        name="inproj_even",
    )(x, scale_pc, shift_pc, g, w_main, w_dt)


def _inproj_odd_kernel(tm, sub, tc, tiles_per_seq, per_chunk_start,
                       x_ref, scale_ref, shift_ref, g_ref, wu_ref, wb_ref, wc_ref, wg_ref, cw_ref, c0_ref,
                       y_ref, cn_ref, h_scr, ebuf, halo):
    i = pl.program_id(0)
    j = pl.program_id(1)

    norm = functools.partial(_modulated_norm_into, x_ref, scale_ref, shift_ref, g_ref, h_scr)
    w0 = cw_ref[0:1, :]
    w1 = cw_ref[1:2, :]
    w2 = cw_ref[2:3, :]
    if not per_chunk_start:
        @pl.when(i % tiles_per_seq == 0)
        def _():
            ebuf[CHUNK + 6:CHUNK + 8, :] = c0_ref[0]

        @pl.when(i % tiles_per_seq != 0)
        def _():
            ebuf[CHUNK + 6:CHUNK + 8, :] = halo[j, 6:8, :]

    def body(with_norm):
        if with_norm:
            norm(0, sub)
        for sb in range(tm // sub):
            hs = h_scr[sb * sub:(sb + 1) * sub, :]
            pu = _dot(hs, wu_ref[...])
            pb = _dot(hs, wb_ref[...])
            pc = _dot(hs, wc_ref[...])
            pg = _dot(hs, wg_ref[...])
            if with_norm and sb + 1 < tm // sub:
                norm((sb + 1) * sub, (sb + 2) * sub)
            for c2 in range(sub // CHUNK):
                cc = sb * (sub // CHUNK) + c2
                r2 = slice(c2 * CHUNK, (c2 + 1) * CHUNK)
                rows = slice(cc * CHUNK, (cc + 1) * CHUNK)
                if per_chunk_start:
                    ebuf[6:8, :] = c0_ref[cc]
                else:
                    ebuf[6:8, :] = ebuf[CHUNK + 6:CHUNK + 8, :]
                ebuf[8:8 + CHUNK, :] = pc[r2] * pu[r2]
                conv = w0 * ebuf[6:6 + CHUNK, :] + w1 * ebuf[7:7 + CHUNK, :] + w2 * ebuf[8:8 + CHUNK, :]
                y_ref[rows, :] = (pb[r2] * conv * _silu(pg[r2])).astype(BF16)
                if per_chunk_start:
                    cn_ref[cc] = ebuf[CHUNK + 6:CHUNK + 8, :]

    pl.when(j == 0)(lambda: body(True))
    pl.when(j != 0)(lambda: body(False))
    if not per_chunk_start:
        tail = ebuf[CHUNK + 6:CHUNK + 8, :]
        halo[j, 6:8, :] = tail
        cn_ref[0] = tail


def _inproj_odd_call(x, scale_pc, shift_pc, g, w_all, li, conv_w, conv0, seq_len, tm, tc):
    r, d = x.shape
    width = conv_w.shape[1]
    nct = width // tc
    nch = tm // CHUNK
    sub = min(SUB_ROWS, tm)
    per_chunk_start = seq_len == CHUNK
    if per_chunk_start:
        tiles_per_seq = 1
        nb = nch
        c0_map = lambda i, j: (i, 0, j)
    else:
        assert seq_len % tm == 0
        tiles_per_seq = seq_len // tm
        nb = 1
        c0_map = lambda i, j: (i // tiles_per_seq, 0, j)
    wspec = lambda q: pl.BlockSpec((None, d, tc), lambda i, j: (li, 0, q * nct + j))
    y, tails = pl.pallas_call(
        functools.partial(_inproj_odd_kernel, tm, sub, tc, tiles_per_seq, per_chunk_start),
        grid=(r // tm, nct),
        in_specs=[
            pl.BlockSpec((tm, d), lambda i, j: (i, 0)),
            pl.BlockSpec((nch, d), lambda i, j: (i, 0)),
            pl.BlockSpec((nch, d), lambda i, j: (i, 0)),
            pl.BlockSpec((1, d), lambda i, j: (0, 0)),
            wspec(0), wspec(1), wspec(2), wspec(3),
            pl.BlockSpec((SHORT_CONV, tc), lambda i, j: (0, j)),
            pl.BlockSpec((nb, SHORT_CONV - 1, tc), c0_map),
        ],
        out_specs=[
            pl.BlockSpec((tm, tc), lambda i, j: (i, j)),
            pl.BlockSpec((nb, SHORT_CONV - 1, tc), lambda i, j: (i, 0, j)),
        ],
        out_shape=[
            jax.ShapeDtypeStruct((r, width), BF16),
            jax.ShapeDtypeStruct((r // tm * nb, SHORT_CONV - 1, width), F32),
        ],
        scratch_shapes=[
            pltpu.VMEM((tm, d), BF16),
            pltpu.VMEM((CHUNK + 8, tc), F32),
            pltpu.VMEM((nct, 8, tc), F32),
        ],
        compiler_params=_cparams(("arbitrary", "arbitrary")),
        name="inproj_odd",
    )(x, scale_pc, shift_pc, g, w_all, w_all, w_all, w_all, conv_w, conv0)
    return y, tails[tiles_per_seq - 1::tiles_per_seq]


def _outproj_kernel(tm, sub, y_ref, w_ref, x_ref, gate_ref, np_ref, o_ref):
    npw = np_ref[...]
    for sb in range(tm // sub):
        o = _dot(y_ref[sb * sub:(sb + 1) * sub, :], w_ref[...])
        for c2 in range(sub // CHUNK):
            cc = sb * (sub // CHUNK) + c2
            rows = slice(cc * CHUNK, (cc + 1) * CHUNK)
            oc = o[c2 * CHUNK:(c2 + 1) * CHUNK, :]
            ms = jnp.mean(oc * oc, axis=-1, keepdims=True)
            r = oc * lax.rsqrt(ms + EPS) * npw
            o_ref[rows, :] = x_ref[rows, :] + gate_ref[cc:cc + 1, :] * r


def _outproj_call(y, w_all, li, x, gate_pc, npost, tm):
    r, k = y.shape
    d = x.shape[1]
    nch = tm // CHUNK
    sub = min(SUB_ROWS, tm)
    return pl.pallas_call(
        functools.partial(_outproj_kernel, tm, sub),
        grid=(r // tm,),
        in_specs=[
            pl.BlockSpec((tm, k), lambda i: (i, 0)),
            pl.BlockSpec((None, k, d), lambda i: (li, 0, 0), pipeline_mode=pl.Buffered(1)),
            pl.BlockSpec((tm, d), lambda i: (i, 0)),
            pl.BlockSpec((nch, d), lambda i: (i, 0)),
            pl.BlockSpec((1, d), lambda i: (0, 0)),
        ],
        out_specs=pl.BlockSpec((tm, d), lambda i: (i, 0)),
        out_shape=jax.ShapeDtypeStruct((r, d), F32),
        compiler_params=_cparams(("arbitrary",)),
        name="outproj",
    )(y, w_all, x, gate_pc, npost)


def _write_kv_slot(kv, buf, base):
    lane = lax.broadcasted_iota(jnp.int32, (CHUNK, LANES), 1)
    lo = lane < ATT_HEAD_DIM
    for m in range(kv.shape[1] // LANES):
        blk = kv[:, m * LANES:(m + 1) * LANES]
        sw = pltpu.roll(blk, ATT_HEAD_DIM, 1)
        pieces = (
            (2 * m, jnp.where(lo, blk, 0.0), jnp.where(lo, 0.0, sw)),
            (2 * m + 1, jnp.where(lo, sw, 0.0), jnp.where(lo, 0.0, blk)),
        )
        for head, low, high in pieces:
            buf[head, pl.ds(base, CHUNK), 0:LANES] = low.astype(BF16)
            buf[head, pl.ds(base + CHUNK, CHUNK), 0:LANES] = high.astype(BF16)


def _mixer_chunk(first, dims, c, nc, maybe_first, maybe_last, write_y,
                 proj_ref, pdt_ref, conv0_ref, ssm0_ref, k0_ref, v0_ref,
                 convw_ref, convb_ref, dtb_ref, alog_ref, dskip_ref, normw_ref, sink_ref, e_ref,
                 convn_ref, ssmn_ref, kn_ref, vn_ref,
                 xext, xs_s, bc_s, ht_s, kbd, vbd):
    inner, gn, kvd = dims
    gw = inner // SSM_GROUPS
    off_xbc = inner
    off_q = off_xbc + inner + 2 * gn
    off_k = off_q + inner
    off_v = off_k + kvd
    off_g = off_v + kvd
    n_tr = inner // LANES

    lane64 = lax.broadcasted_iota(jnp.int32, (CHUNK, LANES), 1) < ATT_HEAD_DIM
    lane128 = lax.broadcasted_iota(jnp.int32, (2 * CHUNK, LANES), 1) < ATT_HEAD_DIM

    def _init():
        xext[5:8, :] = conv0_ref[0]
        for t in range(n_tr):
            ht_s[:, t * LANES:(t + 1) * LANES] = ssm0_ref[0, 2 * t:2 * t + 2].reshape(LANES, SSM_STATE).T
        for s in range(WINDOW // CHUNK):
            _write_kv_slot(k0_ref[0, s * CHUNK:(s + 1) * CHUNK, :], kbd, s * LANES)
            _write_kv_slot(v0_ref[0, s * CHUNK:(s + 1) * CHUNK, :], vbd, s * LANES)
        rowi = lax.broadcasted_iota(jnp.int32, (3 * LANES, LANES), 0)
        lanei = lax.broadcasted_iota(jnp.int32, (3 * LANES, LANES), 1)
        ones_bd = jnp.where((rowi % LANES < CHUNK) == (lanei < ATT_HEAD_DIM), 1.0, 0.0).astype(BF16)
        for h in range(ATT_KV_HEADS):
            vbd[h, :, LANES:2 * LANES] = ones_bd

    if maybe_first:
        pl.when(c == 0)(_init)

    def pj(cols):
        return proj_ref[:, cols].astype(F32)

    k_new = pj(slice(off_k, off_v))
    v_new = pj(slice(off_v, off_g))
    base = pl.multiple_of(lax.rem(c + 2, 3) * LANES, LANES)
    _write_kv_slot(k_new, kbd, base)
    _write_kv_slot(v_new, vbd, base)

    neg_inf = jnp.float32(-jnp.inf)
    qscale = ATT_HEAD_DIM ** -0.5

    def attn_head(kh):
        qa = slice(off_q + 2 * kh * LANES, off_q + (2 * kh + 1) * LANES)
        qb = slice(off_q + (2 * kh + 1) * LANES, off_q + (2 * kh + 2) * LANES)
        qs = (jnp.concatenate([pj(qa), pj(qb)], axis=0) * qscale).astype(BF16)
        s = _dot_nt(qs, kbd[kh])
        sb = [s[:, t * LANES:(t + 1) * LANES] for t in range(3)]
        if first:
            sb[0] = jnp.where(c == 0, neg_inf, sb[0])
            sb[1] = jnp.where(c <= 1, neg_inf, sb[1])
        mb = jnp.maximum(jnp.maximum(sb[0], sb[1]), sb[2])
        m_lo = jnp.max(jnp.where(lane128, mb, neg_inf), axis=-1, keepdims=True)
        m_hi = jnp.max(jnp.where(lane128, neg_inf, mb), axis=-1, keepdims=True)
        sink = jnp.concatenate([jnp.broadcast_to(sink_ref[kh, 0:1, :], (CHUNK, LANES)),
                                jnp.broadcast_to(sink_ref[kh, 1:2, :], (CHUNK, LANES))], axis=0)
        mx = jnp.maximum(jnp.where(lane128, m_lo, m_hi), sink)
        p = jnp.concatenate([jnp.exp(x - mx) for x in sb], axis=1).astype(BF16)
        ov = _dot(p, vbd[kh])
        denom = ov[:, LANES:2 * LANES] + jnp.exp(sink - mx)
        o = ov[:, 0:LANES] / denom
        ga = slice(off_g + 2 * kh * LANES, off_g + (2 * kh + 1) * LANES)
        gb = slice(off_g + (2 * kh + 1) * LANES, off_g + (2 * kh + 2) * LANES)
        ya = slice(inner + 2 * kh * LANES, inner + (2 * kh + 1) * LANES)
        yb = slice(inner + (2 * kh + 1) * LANES, inner + (2 * kh + 2) * LANES)
        write_y(ya, (o[0:CHUNK] * _silu(pj(ga))).astype(BF16))
        write_y(yb, (o[CHUNK:2 * CHUNK] * _silu(pj(gb))).astype(BF16))


    xext[8:8 + CHUNK, :] = pj(slice(off_xbc, off_q))
    piece = 512
    for t in range((inner + 2 * gn) // piece):
        for kh in ATT_HEADS_AFTER_CONV_PIECE.get(t, ()):
            attn_head(kh)
        cs = slice(t * piece, (t + 1) * piece)
        acc = convb_ref[:, cs] + convw_ref[3:4, cs] * xext[8:8 + CHUNK, cs]
        for i in range(SSM_CONV - 1):
            acc = acc + convw_ref[i:i + 1, cs] * xext[5 + i:5 + i + CHUNK, cs]
        act = _silu(acc)
        if t * piece < inner:
            xs_s[:, cs] = act
        else:
            bc_s[:, t * piece - inner:(t + 1) * piece - inner] = act
    tail = xext[CHUNK + 5:CHUNK + 8, :]
    convn_ref[0] = tail
    xext[5:8, :] = tail

    dt = jax.nn.softplus(pdt_ref[...] + dtb_ref[...])
    da = dt * (-jnp.exp(alog_ref[...]))
    ri = lax.broadcasted_iota(jnp.int32, (CHUNK, CHUNK), 0)
    ci = lax.broadcasted_iota(jnp.int32, (CHUNK, CHUNK), 1)
    causal = ri >= ci
    tri = jnp.where(causal, 1.0, 0.0).astype(BF16)
    d_hi = da.astype(BF16)
    r1 = da - d_hi.astype(F32)
    d_mid = r1.astype(BF16)
    d_lo = (r1 - d_mid.astype(F32)).astype(BF16)
    a_cum = _dot(tri, d_hi) + _dot(tri, d_mid) + _dot(tri, d_lo)
    a_last = a_cum[CHUNK - 1:CHUNK, :]
    stack = jnp.concatenate([jnp.exp(a_cum), dt, jnp.exp(a_last - a_cum)], axis=0)
    s_hi = stack.astype(BF16)
    s_lo = (stack - s_hi.astype(F32)).astype(BF16)
    act_t = jnp.concatenate([a_cum, jnp.zeros((CHUNK, LANES), F32)], axis=0).T

    pairs_per_group = gw // LANES

    def ssd_group(g):
        gcs = slice(g * gw, (g + 1) * gw)
        ex = _dot(s_hi, e_ref[:, gcs]) + _dot(s_lo, e_ref[:, gcs])
        ea = ex[0:CHUNK]
        xs = xs_s[:, gcs]
        xdt = xs * ex[CHUNK:2 * CHUNK]
        xdt_b = xdt.astype(BF16)
        xsc_b = (xdt * ex[2 * CHUNK:3 * CHUNK]).astype(BF16)
        bg = bc_s[:, g * SSM_STATE:(g + 1) * SSM_STATE].astype(BF16)
        cg = bc_s[:, gn + g * SSM_STATE:gn + (g + 1) * SSM_STATE].astype(BF16)
        cb = _dot_nt(cg, bg)
        h_prev = ht_s[:, gcs]
        y_off = _dot(cg, h_prev.astype(BF16)) * ea
        ht_s[:, gcs] = h_prev * ea[CHUNK - 1:CHUNK, :] + _dot_tn(bg, xsc_b)
        y_pairs = []
        for pm in range(pairs_per_group):
            m = g * pairs_per_group + pm
            ps = slice(m * LANES, (m + 1) * LANES)
            pl_ = slice(pm * LANES, (pm + 1) * LANES)
            halves = []
            for e in range(2):
                h = 2 * m + e
                seg = a_cum[:, h:h + 1] - act_t[h:h + 1, 0:CHUNK]
                dec = jnp.exp(jnp.where(causal, seg, -jnp.inf))
                halves.append(_dot((cb * dec).astype(BF16), xdt_b[:, pl_]))
            yd = jnp.where(lane64, halves[0], halves[1])
            y_pairs.append(yd + y_off[:, pl_] + dskip_ref[:, ps] * xs[:, pl_])
        yg = jnp.concatenate(y_pairs, axis=1) * _silu(pj(gcs))
        yg = yg * lax.rsqrt(jnp.mean(yg * yg, axis=-1, keepdims=True) + EPS)
        write_y(gcs, (yg * normw_ref[:, gcs]).astype(BF16))

    for g in range(SSM_GROUPS):
        ssd_group(g)
        for kh in ATT_HEADS_AFTER_GROUP[g]:
            attn_head(kh)

    def _window_from_state():
        kn_ref[0, 0:CHUNK, :] = k0_ref[0, CHUNK:2 * CHUNK, :]
        vn_ref[0, 0:CHUNK, :] = v0_ref[0, CHUNK:2 * CHUNK, :]

    def _window_shift():
        kn_ref[0, 0:CHUNK, :] = kn_ref[0, CHUNK:2 * CHUNK, :]
        vn_ref[0, 0:CHUNK, :] = vn_ref[0, CHUNK:2 * CHUNK, :]

    if maybe_first:
        pl.when(c == 0)(_window_from_state)
        pl.when(c != 0)(_window_shift)
    else:
        _window_shift()
    kn_ref[0, CHUNK:2 * CHUNK, :] = pj(slice(off_k, off_v))
    vn_ref[0, CHUNK:2 * CHUNK, :] = pj(slice(off_v, off_g))

    def _final_state():
        for t in range(n_tr):
            ssmn_ref[0, 2 * t:2 * t + 2] = ht_s[:, t * LANES:(t + 1) * LANES].T.reshape(
                2, SSM_HEAD_DIM, SSM_STATE)

    if maybe_last:
        pl.when(c == nc - 1)(_final_state)


N_MIXER_INPUTS = 14


def _even_mixer_kernel(first, dims, cps, *refs):
    ins, y_ref, rest = refs[:N_MIXER_INPUTS], refs[N_MIXER_INPUTS], refs[N_MIXER_INPUTS + 1:]
    proj_ref, pdt_ref = ins[:2]
    for u in range(cps):
        rows = pl.ds(u * CHUNK, CHUNK)

        def write_y(cols, val, rows=rows):
            y_ref[rows, cols] = val

        _mixer_chunk(first, dims, pl.program_id(1) * cps + u, pl.num_programs(1) * cps,
                     u == 0, u == cps - 1, write_y,
                     proj_ref.at[rows], pdt_ref.at[rows], *ins[2:], *rest)


def _mixer_specs(rows, nproj, conv_dim, n_heads, kvd, inner, row, per_b3, per_b4, const2, const3):
    ins = [
        pl.BlockSpec((rows, nproj), row),
        pl.BlockSpec((rows, LANES), row),
        pl.BlockSpec((1, SSM_CONV - 1, conv_dim), per_b3),
        pl.BlockSpec((1, n_heads, SSM_HEAD_DIM, SSM_STATE), per_b4),
        pl.BlockSpec((1, WINDOW, kvd), per_b3),
        pl.BlockSpec((1, WINDOW, kvd), per_b3),
        pl.BlockSpec((SSM_CONV, conv_dim), const2),
        pl.BlockSpec((1, conv_dim), const2),
        pl.BlockSpec((1, LANES), const2),
        pl.BlockSpec((1, LANES), const2),
        pl.BlockSpec((1, inner), const2),
        pl.BlockSpec((1, inner), const2),
        pl.BlockSpec((ATT_KV_HEADS, 2, LANES), const3),
        pl.BlockSpec((LANES, inner), const2),
    ]
    assert len(ins) == N_MIXER_INPUTS
    state_outs = [
        pl.BlockSpec((1, SSM_CONV - 1, conv_dim), per_b3),
        pl.BlockSpec((1, n_heads, SSM_HEAD_DIM, SSM_STATE), per_b4),
        pl.BlockSpec((1, WINDOW, kvd), per_b3),
        pl.BlockSpec((1, WINDOW, kvd), per_b3),
    ]
    return ins, state_outs


def _mixer_state_shapes(nbatch, conv_dim, n_heads, kvd):
    return [
        jax.ShapeDtypeStruct((nbatch, SSM_CONV - 1, conv_dim), F32),
        jax.ShapeDtypeStruct((nbatch, n_heads, SSM_HEAD_DIM, SSM_STATE), F32),
        jax.ShapeDtypeStruct((nbatch, WINDOW, kvd), F32),
        jax.ShapeDtypeStruct((nbatch, WINDOW, kvd), F32),
    ]


def _mixer_scratch(conv_dim, inner, gn):
    return [
        pltpu.VMEM((CHUNK + 8, conv_dim), F32),
        pltpu.VMEM((CHUNK, inner), F32),
        pltpu.VMEM((CHUNK, 2 * gn), F32),
        pltpu.VMEM((SSM_STATE, inner), F32),
        pltpu.VMEM((ATT_KV_HEADS, 3 * LANES, LANES), BF16),
        pltpu.VMEM((ATT_KV_HEADS, 3 * LANES, 2 * LANES), BF16),
    ]


def _even_mixer_call(first, proj, pdt, conv0, ssm0, k0, v0, convw, convb, dtb, alog, dskip_x, normw,
                     sink_t, expand, nbatch, seq_len):
    nc = seq_len // CHUNK
    nproj = proj.shape[1]
    inner = normw.shape[1]
    conv_dim = convw.shape[1]
    gn = (conv_dim - inner) // 2
    kvd = k0.shape[2]
    n_heads = inner // SSM_HEAD_DIM
    cps = MIXER_CHUNKS_PER_STEP if nc % MIXER_CHUNKS_PER_STEP == 0 else 1
    nsteps = nc // cps
    row = lambda b, s: (b * nsteps + s, 0)
    ins, state_outs = _mixer_specs(cps * CHUNK, nproj, conv_dim, n_heads, kvd, inner, row,
                                   lambda b, s: (b, 0, 0), lambda b, s: (b, 0, 0, 0),
                                   lambda b, s: (0, 0), lambda b, s: (0, 0, 0))
    return pl.pallas_call(
        functools.partial(_even_mixer_kernel, first, (inner, gn, kvd), cps),
        grid=(nbatch, nsteps),
        in_specs=ins,
        out_specs=[pl.BlockSpec((cps * CHUNK, 2 * inner), row)] + state_outs,
        out_shape=[jax.ShapeDtypeStruct((nbatch * seq_len, 2 * inner), BF16)]
        + _mixer_state_shapes(nbatch, conv_dim, n_heads, kvd),
        scratch_shapes=_mixer_scratch(conv_dim, inner, gn),
        compiler_params=_cparams(("arbitrary", "arbitrary")),
        name="even_mixer",
    )(proj, pdt, conv0, ssm0, k0, v0, convw, convb, dtb, alog, dskip_x, normw, sink_t, expand)


def _pad_lanes(v):
    return jnp.pad(v, ((0, 0), (0, LANES - v.shape[1])))


def kernel(x_prompt, x_sample, c_prompt, c_sample, cache_k, cache_v, state_conv_a, state_ssm, state_conv_c,
           w_ada, b_ada, norm_pre, norm_post, w_in_even, conv_a_w, conv_a_b, dt_bias, a_log, d_skip, norm_ssm,
           sinks, w_out_even, w_in_odd, conv_c_w, w_out_odd):
    bp, lp, d = x_prompt.shape
    bs, ls, _ = x_sample.shape
    depth = w_ada.shape[0]
    inner = norm_ssm.shape[1]
    n_heads = dt_bias.shape[1]
    conv_dim = conv_a_w.shape[2]
    kvd = cache_k.shape[3] * cache_k.shape[4]
    width = conv_c_w.shape[2]
    assert ls == CHUNK and lp % CHUNK == 0 and inner == d and n_heads <= LANES

    mod = _mod_call(jnp.concatenate([c_prompt, c_sample], axis=0), w_ada, b_ada)

    xp = x_prompt.reshape(bp * lp, d)
    xs = x_sample.reshape(bs * ls, d)
    tm_p = min(TM_IN, lp)
    tm_s = bs * ls
    tm_out_p = min(TM_OUT, lp)
    expand = (jnp.arange(LANES)[:, None] == (jnp.arange(inner) // SSM_HEAD_DIM)[None, :]).astype(BF16)
    zeros_p = dict(
        conv_a=jnp.zeros((bp, SSM_CONV - 1, conv_dim), F32),
        ssm=jnp.zeros((bp, n_heads, SSM_HEAD_DIM, SSM_STATE), F32),
        kv=jnp.zeros((bp, WINDOW, kvd), F32),
        conv_c=jnp.zeros((bp, SHORT_CONV - 1, width), F32),
    )
    outs_p = dict(k=[], v=[], ca=[], ssm=[], cc=[])
    outs_s = dict(k=[], v=[], ca=[], ssm=[], cc=[])
    dt_lo = inner + conv_dim
    dt_hi = dt_lo + n_heads
    tc = TC_ODD
    w_main_all, w_dt_all = _cast_even_call(w_in_even, dt_lo, dt_hi, 128)
    w_in_odd_b = _cast_call(w_in_odd, 128)
    w_out_even_b = _cast_call(w_out_even, 512)
    w_out_odd_b = _cast_call(w_out_odd, 512)

    for layer in range(depth):
        i = layer // 2
        shift, scale, gate = jnp.split(mod[layer], 3, axis=-1)
        rep = lambda v: jnp.repeat(v[:bp], lp // CHUNK, axis=0)
        g_pre = norm_pre[layer][None, :]
        g_post = norm_post[layer][None, :]
        if layer % 2 == 0:
            mixer_params = (
                conv_a_w[i], conv_a_b[i][None, :], _pad_lanes(dt_bias[i][None, :]), _pad_lanes(a_log[i][None, :]),
                jnp.repeat(d_skip[i], SSM_HEAD_DIM)[None, :], norm_ssm[i][None, :],
                jnp.repeat(sinks[i], ATT_HEAD_DIM).reshape(ATT_KV_HEADS, 2, LANES), expand)
            proj, pdt = _inproj_even_call(xp, rep(scale), rep(shift), g_pre, w_main_all, w_dt_all, i, tm_p, TN_IN)
            y, ca, ssm, kn, vn = _even_mixer_call(True, proj, pdt, zeros_p["conv_a"], zeros_p["ssm"],
                                                  zeros_p["kv"], zeros_p["kv"], *mixer_params, bp, lp)
            xp = _outproj_call(y, w_out_even_b, i, xp, rep(gate), g_post, tm_out_p)
            outs_p["k"].append(kn); outs_p["v"].append(vn); outs_p["ca"].append(ca); outs_p["ssm"].append(ssm)
            proj, pdt = _inproj_even_call(xs, scale[bp:], shift[bp:], g_pre, w_main_all, w_dt_all, i, tm_s, TN_IN)
            y, ca, ssm, kn, vn = _even_mixer_call(False, proj, pdt, state_conv_a[i], state_ssm[i],
                                                  cache_k[i].reshape(bs, WINDOW, kvd),
                                                  cache_v[i].reshape(bs, WINDOW, kvd), *mixer_params, bs, ls)
            xs = _outproj_call(y, w_out_even_b, i, xs, gate[bp:], g_post, tm_s)
            outs_s["k"].append(kn); outs_s["v"].append(vn); outs_s["ca"].append(ca); outs_s["ssm"].append(ssm)
        else:
            y, cc = _inproj_odd_call(xp, rep(scale), rep(shift), g_pre, w_in_odd_b, i, conv_c_w[i],
                                     zeros_p["conv_c"], lp, tm_p, tc)
            xp = _outproj_call(y, w_out_odd_b, i, xp, rep(gate), g_post, tm_out_p)
            outs_p["cc"].append(cc)
            y, cc = _inproj_odd_call(xs, scale[bp:], shift[bp:], g_pre, w_in_odd_b, i, conv_c_w[i],
                                     state_conv_c[i], ls, tm_s, tc)
            xs = _outproj_call(y, w_out_odd_b, i, xs, gate[bp:], g_post, tm_s)
            outs_s["cc"].append(cc)

    kv_shape_p = (len(outs_p["k"]), bp, WINDOW) + cache_k.shape[3:]
    kv_shape_s = (len(outs_s["k"]), bs, WINDOW) + cache_k.shape[3:]
    return (xp.reshape(bp, lp, d), xs.reshape(bs, ls, d),
            jnp.stack(outs_p["k"]).reshape(kv_shape_p), jnp.stack(outs_p["v"]).reshape(kv_shape_p),
            jnp.stack(outs_p["ca"]), jnp.stack(outs_p["ssm"]), jnp.stack(outs_p["cc"]),
            jnp.stack(outs_s["k"]).reshape(kv_shape_s), jnp.stack(outs_s["v"]).reshape(kv_shape_s),
            jnp.stack(outs_s["ca"]), jnp.stack(outs_s["ssm"]), jnp.stack(outs_s["cc"]))
```

```python
import functools

import jax
import jax.numpy as jnp
from jax import lax
from jax.experimental import pallas as pl
from jax.experimental.pallas import tpu as pltpu

F32 = jnp.float32
BF16 = jnp.bfloat16

CHUNK = 64
WINDOW = 128
EPS = 1e-6
SSM_HEAD_DIM = 64
SSM_GROUPS = 4
SSM_STATE = 128
SSM_CONV = 4
ATT_HEAD_DIM = 64
ATT_KV_HEADS = 8
SHORT_CONV = 3
LANES = 128

VMEM_LIMIT = 56 * 1024 * 1024
VMEM_LIMIT_EVEN_IN = 60 * 1024 * 1024
TM_IN = 1024
TN_IN = 2048
TM_OUT = 512
TC_ODD = 512
MIXER_CHUNKS_PER_STEP = 2
ATT_HEADS_AFTER_CONV_PIECE = {}
ATT_HEADS_AFTER_GROUP = ((0, 1), (2, 3), (4, 5), (6, 7))
SUB_ROWS = 256


def _cparams(sem, vmem=VMEM_LIMIT):
    return pltpu.CompilerParams(dimension_semantics=sem, vmem_limit_bytes=vmem)


def _dot(a, b):
    return jnp.dot(a, b, preferred_element_type=F32)


def _dot_nt(a, b):
    return lax.dot_general(a, b, (((1,), (1,)), ((), ())), preferred_element_type=F32)


def _dot_tn(a, b):
    return lax.dot_general(a, b, (((0,), (0,)), ((), ())), preferred_element_type=F32)


def _silu(x):
    return x * jax.nn.sigmoid(x)


def _cast_kernel(x_ref, o_ref):
    o_ref[...] = x_ref[...].astype(BF16)


def _cast_call(w, tr):
    n, r, c = w.shape
    return pl.pallas_call(
        _cast_kernel,
        grid=(n, r // tr),
        in_specs=[pl.BlockSpec((None, tr, c), lambda l, i: (l, i, 0))],
        out_specs=pl.BlockSpec((None, tr, c), lambda l, i: (l, i, 0)),
        out_shape=jax.ShapeDtypeStruct(w.shape, BF16),
        compiler_params=_cparams(("arbitrary", "arbitrary")),
        name="cast_bf16",
    )(w)


def _cast_even_kernel(dt_lo, dt_hi, x_ref, o_ref, odt_ref):
    n_main = o_ref.shape[1]
    o_ref[:, 0:dt_lo] = x_ref[:, 0:dt_lo].astype(BF16)
    o_ref[:, dt_lo:n_main] = x_ref[:, dt_hi:dt_hi + n_main - dt_lo].astype(BF16)
    odt_ref[...] = jnp.zeros(odt_ref.shape, BF16)
    odt_ref[:, 0:dt_hi - dt_lo] = x_ref[:, dt_lo:dt_hi].astype(BF16)


def _cast_even_call(w, dt_lo, dt_hi, tr):
    n, r, c = w.shape
    n_main = c - (dt_hi - dt_lo)
    main, dt = pl.pallas_call(
        functools.partial(_cast_even_kernel, dt_lo, dt_hi),
        grid=(n * r // tr,),
        in_specs=[pl.BlockSpec((tr, c), lambda i: (i, 0))],
        out_specs=[pl.BlockSpec((tr, n_main), lambda i: (i, 0)),
                   pl.BlockSpec((tr, LANES), lambda i: (i, 0))],
        out_shape=[jax.ShapeDtypeStruct((n * r, n_main), BF16),
                   jax.ShapeDtypeStruct((n * r, LANES), BF16)],
        compiler_params=_cparams(("arbitrary",)),
        name="cast_even",
    )(w.reshape(n * r, c))
    return main.reshape(n, r, n_main), dt.reshape(n, r, LANES)


def _mod_kernel(c_ref, w_ref, b_ref, o_ref):
    sc = _silu(c_ref[...]).astype(BF16)
    o_ref[0] = _dot(sc, w_ref[0].astype(BF16)) + b_ref[0]


def _mod_call(c_all, w_ada, b_ada):
    depth, d, d3 = w_ada.shape
    nb = c_all.shape[0]
    tn = 1024
    return pl.pallas_call(
        _mod_kernel,
        grid=(depth, d3 // tn),
        in_specs=[
            pl.BlockSpec((nb, d), lambda l, j: (0, 0)),
            pl.BlockSpec((1, d, tn), lambda l, j: (l, 0, j)),
            pl.BlockSpec((1, 1, tn), lambda l, j: (l, 0, j)),
        ],
        out_specs=pl.BlockSpec((1, nb, tn), lambda l, j: (l, 0, j)),
        out_shape=jax.ShapeDtypeStruct((depth, nb, d3), F32),
        compiler_params=_cparams(("arbitrary", "arbitrary")),
        name="adaln_mod",
    )(c_all, w_ada, b_ada.reshape(depth, 1, d3))


def _modulated_norm_into(x_ref, scale_ref, shift_ref, g_ref, h_scr, row0, row1):
    g = g_ref[...]
    for cc in range(row0 // CHUNK, row1 // CHUNK):
        rows = slice(cc * CHUNK, (cc + 1) * CHUNK)
        x = x_ref[rows, :]
        ms = jnp.mean(x * x, axis=-1, keepdims=True)
        y = x * lax.rsqrt(ms + EPS) * g
        h = y * (1.0 + scale_ref[cc:cc + 1, :]) + shift_ref[cc:cc + 1, :]
        h_scr[rows, :] = h.astype(BF16)


def _inproj_even_kernel(tm, sub, x_ref, scale_ref, shift_ref, g_ref, w_ref, wdt_ref, o_ref, odt_ref, h_scr):
    j = pl.program_id(1)

    @pl.when(j == 0)
    def _():
        norm = functools.partial(_modulated_norm_into, x_ref, scale_ref, shift_ref, g_ref, h_scr)
        norm(0, sub)
        for sb in range(tm // sub):
            rows = slice(sb * sub, (sb + 1) * sub)
            o_ref[rows, :] = _dot(h_scr[rows, :], w_ref[...])
            odt_ref[rows, :] = _dot(h_scr[rows, :], wdt_ref[...])
            if sb + 1 < tm // sub:
                norm((sb + 1) * sub, (sb + 2) * sub)

    @pl.when(j != 0)
    def _():
        o_ref[...] = _dot(h_scr[...], w_ref[...])


def _inproj_even_call(x, scale_pc, shift_pc, g, w_main, w_dt, li, tm, tn):
    r, d = x.shape
    n = w_main.shape[2]
    nch = tm // CHUNK
    return pl.pallas_call(
        functools.partial(_inproj_even_kernel, tm, min(SUB_ROWS, tm)),
        grid=(r // tm, n // tn),
        in_specs=[
            pl.BlockSpec((tm, d), lambda i, j: (i, 0)),
            pl.BlockSpec((nch, d), lambda i, j: (i, 0)),
            pl.BlockSpec((nch, d), lambda i, j: (i, 0)),
            pl.BlockSpec((1, d), lambda i, j: (0, 0)),
            pl.BlockSpec((None, d, tn), lambda i, j: (li, 0, j)),
            pl.BlockSpec((None, d, LANES), lambda i, j: (li, 0, 0)),
        ],
        out_specs=[
            pl.BlockSpec((tm, tn), lambda i, j: (i, j)),
            pl.BlockSpec((tm, LANES), lambda i, j: (i, 0)),
        ],
        out_shape=[
            jax.ShapeDtypeStruct((r, n), F32),
            jax.ShapeDtypeStruct((r, LANES), F32),
        ],
        scratch_shapes=[pltpu.VMEM((tm, d), BF16)],
        compiler_params=_cparams(("arbitrary", "arbitrary"), VMEM_LIMIT_EVEN_IN),
        name="inproj_even",
    )(x, scale_pc, shift_pc, g, w_main, w_dt)


def _inproj_odd_kernel(tm, sub, tc, tiles_per_seq, per_chunk_start,
                       x_ref, scale_ref, shift_ref, g_ref, wu_ref, wb_ref, wc_ref, wg_ref, cw_ref, c0_ref,
                       y_ref, cn_ref, h_scr, ebuf, halo):
    i = pl.program_id(0)
    j = pl.program_id(1)

    norm = functools.partial(_modulated_norm_into, x_ref, scale_ref, shift_ref, g_ref, h_scr)
    w0 = cw_ref[0:1, :]
    w1 = cw_ref[1:2, :]
    w2 = cw_ref[2:3, :]
    if not per_chunk_start:
        @pl.when(i % tiles_per_seq == 0)
        def _():
            ebuf[CHUNK + 6:CHUNK + 8, :] = c0_ref[0]

        @pl.when(i % tiles_per_seq != 0)
        def _():
            ebuf[CHUNK + 6:CHUNK + 8, :] = halo[j, 6:8, :]

    def body(with_norm):
        if with_norm:
            norm(0, sub)
        for sb in range(tm // sub):
            hs = h_scr[sb * sub:(sb + 1) * sub, :]
            pu = _dot(hs, wu_ref[...])
            pb = _dot(hs, wb_ref[...])
            pc = _dot(hs, wc_ref[...])
            pg = _dot(hs, wg_ref[...])
            if with_norm and sb + 1 < tm // sub:
                norm((sb + 1) * sub, (sb + 2) * sub)
            for c2 in range(sub // CHUNK):
                cc = sb * (sub // CHUNK) + c2
                r2 = slice(c2 * CHUNK, (c2 + 1) * CHUNK)
                rows = slice(cc * CHUNK, (cc + 1) * CHUNK)
                if per_chunk_start:
                    ebuf[6:8, :] = c0_ref[cc]
                else:
                    ebuf[6:8, :] = ebuf[CHUNK + 6:CHUNK + 8, :]
                ebuf[8:8 + CHUNK, :] = pc[r2] * pu[r2]
                conv = w0 * ebuf[6:6 + CHUNK, :] + w1 * ebuf[7:7 + CHUNK, :] + w2 * ebuf[8:8 + CHUNK, :]
                y_ref[rows, :] = (pb[r2] * conv * _silu(pg[r2])).astype(BF16)
                if per_chunk_start:
                    cn_ref[cc] = ebuf[CHUNK + 6:CHUNK + 8, :]

    pl.when(j == 0)(lambda: body(True))
    pl.when(j != 0)(lambda: body(False))
    if not per_chunk_start:
        tail = ebuf[CHUNK + 6:CHUNK + 8, :]
        halo[j, 6:8, :] = tail
        cn_ref[0] = tail


def _inproj_odd_call(x, scale_pc, shift_pc, g, w_all, li, conv_w, conv0, seq_len, tm, tc):
    r, d = x.shape
    width = conv_w.shape[1]
    nct = width // tc
    nch = tm // CHUNK
    sub = min(SUB_ROWS, tm)
    per_chunk_start = seq_len == CHUNK
    if per_chunk_start:
        tiles_per_seq = 1
        nb = nch
        c0_map = lambda i, j: (i, 0, j)
    else:
        assert seq_len % tm == 0
        tiles_per_seq = seq_len // tm
        nb = 1
        c0_map = lambda i, j: (i // tiles_per_seq, 0, j)
    wspec = lambda q: pl.BlockSpec((None, d, tc), lambda i, j: (li, 0, q * nct + j))
    y, tails = pl.pallas_call(
        functools.partial(_inproj_odd_kernel, tm, sub, tc, tiles_per_seq, per_chunk_start),
        grid=(r // tm, nct),
        in_specs=[
            pl.BlockSpec((tm, d), lambda i, j: (i, 0)),
            pl.BlockSpec((nch, d), lambda i, j: (i, 0)),
            pl.BlockSpec((nch, d), lambda i, j: (i, 0)),
            pl.BlockSpec((1, d), lambda i, j: (0, 0)),
            wspec(0), wspec(1), wspec(2), wspec(3),
            pl.BlockSpec((SHORT_CONV, tc), lambda i, j: (0, j)),
            pl.BlockSpec((nb, SHORT_CONV - 1, tc), c0_map),
        ],
        out_specs=[
            pl.BlockSpec((tm, tc), lambda i, j: (i, j)),
            pl.BlockSpec((nb, SHORT_CONV - 1, tc), lambda i, j: (i, 0, j)),
        ],
        out_shape=[
            jax.ShapeDtypeStruct((r, width), BF16),
            jax.ShapeDtypeStruct((r // tm * nb, SHORT_CONV - 1, width), F32),
        ],
        scratch_shapes=[
            pltpu.VMEM((tm, d), BF16),
            pltpu.VMEM((CHUNK + 8, tc), F32),
            pltpu.VMEM((nct, 8, tc), F32),
        ],
        compiler_params=_cparams(("arbitrary", "arbitrary")),
        name="inproj_odd",
    )(x, scale_pc, shift_pc, g, w_all, w_all, w_all, w_all, conv_w, conv0)
    return y, tails[tiles_per_seq - 1::tiles_per_seq]


def _outproj_kernel(tm, sub, y_ref, w_ref, x_ref, gate_ref, np_ref, o_ref):
    npw = np_ref[...]
    for sb in range(tm // sub):
        o = _dot(y_ref[sb * sub:(sb + 1) * sub, :], w_ref[...])
        for c2 in range(sub // CHUNK):
            cc = sb * (sub // CHUNK) + c2
            rows = slice(cc * CHUNK, (cc + 1) * CHUNK)
            oc = o[c2 * CHUNK:(c2 + 1) * CHUNK, :]
            ms = jnp.mean(oc * oc, axis=-1, keepdims=True)
            r = oc * lax.rsqrt(ms + EPS) * npw
            o_ref[rows, :] = x_ref[rows, :] + gate_ref[cc:cc + 1, :] * r


def _outproj_call(y, w_all, li, x, gate_pc, npost, tm):
    r, k = y.shape
    d = x.shape[1]
    nch = tm // CHUNK
    sub = min(SUB_ROWS, tm)
    return pl.pallas_call(
        functools.partial(_outproj_kernel, tm, sub),
        grid=(r // tm,),
        in_specs=[
            pl.BlockSpec((tm, k), lambda i: (i, 0)),
            pl.BlockSpec((None, k, d), lambda i: (li, 0, 0), pipeline_mode=pl.Buffered(1)),
            pl.BlockSpec((tm, d), lambda i: (i, 0)),
            pl.BlockSpec((nch, d), lambda i: (i, 0)),
            pl.BlockSpec((1, d), lambda i: (0, 0)),
        ],
        out_specs=pl.BlockSpec((tm, d), lambda i: (i, 0)),
        out_shape=jax.ShapeDtypeStruct((r, d), F32),
        compiler_params=_cparams(("arbitrary",)),
        name="outproj",
    )(y, w_all, x, gate_pc, npost)


def _write_kv_slot(kv, buf, base):
    lane = lax.broadcasted_iota(jnp.int32, (CHUNK, LANES), 1)
    lo = lane < ATT_HEAD_DIM
    for m in range(kv.shape[1] // LANES):
        blk = kv[:, m * LANES:(m + 1) * LANES]
        sw = pltpu.roll(blk, ATT_HEAD_DIM, 1)
        pieces = (
            (2 * m, jnp.where(lo, blk, 0.0), jnp.where(lo, 0.0, sw)),
            (2 * m + 1, jnp.where(lo, sw, 0.0), jnp.where(lo, 0.0, blk)),
        )
        for head, low, high in pieces:
            buf[head, pl.ds(base, CHUNK), 0:LANES] = low.astype(BF16)
            buf[head, pl.ds(base + CHUNK, CHUNK), 0:LANES] = high.astype(BF16)


def _mixer_chunk(first, dims, c, nc, maybe_first, maybe_last, write_y,
                 proj_ref, pdt_ref, conv0_ref, ssm0_ref, k0_ref, v0_ref,
                 convw_ref, convb_ref, dtb_ref, alog_ref, dskip_ref, normw_ref, sink_ref, e_ref,
                 convn_ref, ssmn_ref, kn_ref, vn_ref,
                 xext, xs_s, bc_s, ht_s, kbd, vbd):
    inner, gn, kvd = dims
    gw = inner // SSM_GROUPS
    off_xbc = inner
    off_q = off_xbc + inner + 2 * gn
    off_k = off_q + inner
    off_v = off_k + kvd
    off_g = off_v + kvd
    n_tr = inner // LANES

    lane64 = lax.broadcasted_iota(jnp.int32, (CHUNK, LANES), 1) < ATT_HEAD_DIM
    lane128 = lax.broadcasted_iota(jnp.int32, (2 * CHUNK, LANES), 1) < ATT_HEAD_DIM

    def _init():
        xext[5:8, :] = conv0_ref[0]
        for t in range(n_tr):
            ht_s[:, t * LANES:(t + 1) * LANES] = ssm0_ref[0, 2 * t:2 * t + 2].reshape(LANES, SSM_STATE).T
        for s in range(WINDOW // CHUNK):
            _write_kv_slot(k0_ref[0, s * CHUNK:(s + 1) * CHUNK, :], kbd, s * LANES)
            _write_kv_slot(v0_ref[0, s * CHUNK:(s + 1) * CHUNK, :], vbd, s * LANES)
        rowi = lax.broadcasted_iota(jnp.int32, (3 * LANES, LANES), 0)
        lanei = lax.broadcasted_iota(jnp.int32, (3 * LANES, LANES), 1)
        ones_bd = jnp.where((rowi % LANES < CHUNK) == (lanei < ATT_HEAD_DIM), 1.0, 0.0).astype(BF16)
        for h in range(ATT_KV_HEADS):
            vbd[h, :, LANES:2 * LANES] = ones_bd

    if maybe_first:
        pl.when(c == 0)(_init)

    def pj(cols):
        return proj_ref[:, cols]

    k_new = pj(slice(off_k, off_v))
    v_new = pj(slice(off_v, off_g))
    base = pl.multiple_of(lax.rem(c + 2, 3) * LANES, LANES)
    _write_kv_slot(k_new, kbd, base)
    _write_kv_slot(v_new, vbd, base)

    neg_inf = jnp.float32(-jnp.inf)
    qscale = ATT_HEAD_DIM ** -0.5

    def attn_head(kh):
        qa = slice(off_q + 2 * kh * LANES, off_q + (2 * kh + 1) * LANES)
        qb = slice(off_q + (2 * kh + 1) * LANES, off_q + (2 * kh + 2) * LANES)
        qs = (jnp.concatenate([pj(qa), pj(qb)], axis=0) * qscale).astype(BF16)
        s = _dot_nt(qs, kbd[kh])
        sb = [s[:, t * LANES:(t + 1) * LANES] for t in range(3)]
        if first:
            sb[0] = jnp.where(c == 0, neg_inf, sb[0])
            sb[1] = jnp.where(c <= 1, neg_inf, sb[1])
        mb = jnp.maximum(jnp.maximum(sb[0], sb[1]), sb[2])
        m_lo = jnp.max(jnp.where(lane128, mb, neg_inf), axis=-1, keepdims=True)
        m_hi = jnp.max(jnp.where(lane128, neg_inf, mb), axis=-1, keepdims=True)
        sink = jnp.concatenate([jnp.broadcast_to(sink_ref[kh, 0:1, :], (CHUNK, LANES)),
                                jnp.broadcast_to(sink_ref[kh, 1:2, :], (CHUNK, LANES))], axis=0)
        mx = jnp.maximum(jnp.where(lane128, m_lo, m_hi), sink)
        p = jnp.concatenate([jnp.exp(x - mx) for x in sb], axis=1).astype(BF16)
        ov = _dot(p, vbd[kh])
        denom = ov[:, LANES:2 * LANES] + jnp.exp(sink - mx)
        o = ov[:, 0:LANES] / denom
        ga = slice(off_g + 2 * kh * LANES, off_g + (2 * kh + 1) * LANES)
        gb = slice(off_g + (2 * kh + 1) * LANES, off_g + (2 * kh + 2) * LANES)
        ya = slice(inner + 2 * kh * LANES, inner + (2 * kh + 1) * LANES)
        yb = slice(inner + (2 * kh + 1) * LANES, inner + (2 * kh + 2) * LANES)
        write_y(ya, (o[0:CHUNK] * _silu(pj(ga))).astype(BF16))
        write_y(yb, (o[CHUNK:2 * CHUNK] * _silu(pj(gb))).astype(BF16))


    xext[8:8 + CHUNK, :] = pj(slice(off_xbc, off_q))
    piece = 512
    for t in range((inner + 2 * gn) // piece):
        for kh in ATT_HEADS_AFTER_CONV_PIECE.get(t, ()):
            attn_head(kh)
        cs = slice(t * piece, (t + 1) * piece)
        acc = convb_ref[:, cs] + convw_ref[3:4, cs] * xext[8:8 + CHUNK, cs]
        for i in range(SSM_CONV - 1):
            acc = acc + convw_ref[i:i + 1, cs] * xext[5 + i:5 + i + CHUNK, cs]
        act = _silu(acc)
        if t * piece < inner:
            xs_s[:, cs] = act
        else:
            bc_s[:, t * piece - inner:(t + 1) * piece - inner] = act
    tail = xext[CHUNK + 5:CHUNK + 8, :]
    convn_ref[0] = tail
    xext[5:8, :] = tail

    dt = jax.nn.softplus(pdt_ref[...] + dtb_ref[...])
    da = dt * (-jnp.exp(alog_ref[...]))
    ri = lax.broadcasted_iota(jnp.int32, (CHUNK, CHUNK), 0)
    ci = lax.broadcasted_iota(jnp.int32, (CHUNK, CHUNK), 1)
    causal = ri >= ci
    tri = jnp.where(causal, 1.0, 0.0).astype(BF16)
    d_hi = da.astype(BF16)
    r1 = da - d_hi.astype(F32)
    d_mid = r1.astype(BF16)
    d_lo = (r1 - d_mid.astype(F32)).astype(BF16)
    a_cum = _dot(tri, d_hi) + _dot(tri, d_mid) + _dot(tri, d_lo)
    a_last = a_cum[CHUNK - 1:CHUNK, :]
    stack = jnp.concatenate([jnp.exp(a_cum), dt, jnp.exp(a_last - a_cum)], axis=0)
    s_hi = stack.astype(BF16)
    s_lo = (stack - s_hi.astype(F32)).astype(BF16)
    act_t = jnp.concatenate([a_cum, jnp.zeros((CHUNK, LANES), F32)], axis=0).T

    pairs_per_group = gw // LANES

    def ssd_group(g):
        gcs = slice(g * gw, (g + 1) * gw)
        ex = _dot(s_hi, e_ref[:, gcs]) + _dot(s_lo, e_ref[:, gcs])
        ea = ex[0:CHUNK]
        xs = xs_s[:, gcs]
        xdt = xs * ex[CHUNK:2 * CHUNK]
        xdt_b = xdt.astype(BF16)
        xsc_b = (xdt * ex[2 * CHUNK:3 * CHUNK]).astype(BF16)
        bg = bc_s[:, g * SSM_STATE:(g + 1) * SSM_STATE].astype(BF16)
        cg = bc_s[:, gn + g * SSM_STATE:gn + (g + 1) * SSM_STATE].astype(BF16)
        cb = _dot_nt(cg, bg)
        h_prev = ht_s[:, gcs]
        y_off = _dot(cg, h_prev.astype(BF16)) * ea
        ht_s[:, gcs] = h_prev * ea[CHUNK - 1:CHUNK, :] + _dot_tn(bg, xsc_b)
        y_pairs = []
        for pm in range(pairs_per_group):
            m = g * pairs_per_group + pm
            ps = slice(m * LANES, (m + 1) * LANES)
            pl_ = slice(pm * LANES, (pm + 1) * LANES)
            halves = []
            for e in range(2):
                h = 2 * m + e
                seg = a_cum[:, h:h + 1] - act_t[h:h + 1, 0:CHUNK]
                dec = jnp.exp(jnp.where(causal, seg, -jnp.inf))
                halves.append(_dot((cb * dec).astype(BF16), xdt_b[:, pl_]))
            yd = jnp.where(lane64, halves[0], halves[1])
            y_pairs.append(yd + y_off[:, pl_] + dskip_ref[:, ps] * xs[:, pl_])
        yg = jnp.concatenate(y_pairs, axis=1) * _silu(pj(gcs))
        yg = yg * lax.rsqrt(jnp.mean(yg * yg, axis=-1, keepdims=True) + EPS)
        write_y(gcs, (yg * normw_ref[:, gcs]).astype(BF16))

    for g in range(SSM_GROUPS):
        ssd_group(g)
        for kh in ATT_HEADS_AFTER_GROUP[g]:
            attn_head(kh)

    def _window_from_state():
        kn_ref[0, 0:CHUNK, :] = k0_ref[0, CHUNK:2 * CHUNK, :]
        vn_ref[0, 0:CHUNK, :] = v0_ref[0, CHUNK:2 * CHUNK, :]

    def _window_shift():
        kn_ref[0, 0:CHUNK, :] = kn_ref[0, CHUNK:2 * CHUNK, :]
        vn_ref[0, 0:CHUNK, :] = vn_ref[0, CHUNK:2 * CHUNK, :]

    if maybe_first:
        pl.when(c == 0)(_window_from_state)
        pl.when(c != 0)(_window_shift)
    else:
        _window_shift()
    kn_ref[0, CHUNK:2 * CHUNK, :] = pj(slice(off_k, off_v))
    vn_ref[0, CHUNK:2 * CHUNK, :] = pj(slice(off_v, off_g))

    def _final_state():
        for t in range(n_tr):
            ssmn_ref[0, 2 * t:2 * t + 2] = ht_s[:, t * LANES:(t + 1) * LANES].T.reshape(
                2, SSM_HEAD_DIM, SSM_STATE)

    if maybe_last:
        pl.when(c == nc - 1)(_final_state)


N_MIXER_INPUTS = 14


def _even_mixer_kernel(first, dims, cps, *refs):
    ins, y_ref, rest = refs[:N_MIXER_INPUTS], refs[N_MIXER_INPUTS], refs[N_MIXER_INPUTS + 1:]
    proj_ref, pdt_ref = ins[:2]
    for u in range(cps):
        rows = pl.ds(u * CHUNK, CHUNK)

        def write_y(cols, val, rows=rows):
            y_ref[rows, cols] = val

        _mixer_chunk(first, dims, pl.program_id(1) * cps + u, pl.num_programs(1) * cps,
                     u == 0, u == cps - 1, write_y,
                     proj_ref.at[rows], pdt_ref.at[rows], *ins[2:], *rest)


def _mixer_specs(rows, nproj, conv_dim, n_heads, kvd, inner, row, per_b3, per_b4, const2, const3):
    ins = [
        pl.BlockSpec((rows, nproj), row),
        pl.BlockSpec((rows, LANES), row),
        pl.BlockSpec((1, SSM_CONV - 1, conv_dim), per_b3),
        pl.BlockSpec((1, n_heads, SSM_HEAD_DIM, SSM_STATE), per_b4),
        pl.BlockSpec((1, WINDOW, kvd), per_b3),
        pl.BlockSpec((1, WINDOW, kvd), per_b3),
        pl.BlockSpec((SSM_CONV, conv_dim), const2),
        pl.BlockSpec((1, conv_dim), const2),
        pl.BlockSpec((1, LANES), const2),
        pl.BlockSpec((1, LANES), const2),
        pl.BlockSpec((1, inner), const2),
        pl.BlockSpec((1, inner), const2),
        pl.BlockSpec((ATT_KV_HEADS, 2, LANES), const3),
        pl.BlockSpec((LANES, inner), const2),
    ]
    assert len(ins) == N_MIXER_INPUTS
    state_outs = [
        pl.BlockSpec((1, SSM_CONV - 1, conv_dim), per_b3),
        pl.BlockSpec((1, n_heads, SSM_HEAD_DIM, SSM_STATE), per_b4),
        pl.BlockSpec((1, WINDOW, kvd), per_b3),
        pl.BlockSpec((1, WINDOW, kvd), per_b3),
    ]
    return ins, state_outs


def _mixer_state_shapes(nbatch, conv_dim, n_heads, kvd):
    return [
        jax.ShapeDtypeStruct((nbatch, SSM_CONV - 1, conv_dim), F32),
        jax.ShapeDtypeStruct((nbatch, n_heads, SSM_HEAD_DIM, SSM_STATE), F32),
        jax.ShapeDtypeStruct((nbatch, WINDOW, kvd), F32),
        jax.ShapeDtypeStruct((nbatch, WINDOW, kvd), F32),
    ]


def _mixer_scratch(conv_dim, inner, gn):
    return [
        pltpu.VMEM((CHUNK + 8, conv_dim), F32),
        pltpu.VMEM((CHUNK, inner), F32),
        pltpu.VMEM((CHUNK, 2 * gn), F32),
        pltpu.VMEM((SSM_STATE, inner), F32),
        pltpu.VMEM((ATT_KV_HEADS, 3 * LANES, LANES), BF16),
        pltpu.VMEM((ATT_KV_HEADS, 3 * LANES, 2 * LANES), BF16),
    ]


def _even_mixer_call(first, proj, pdt, conv0, ssm0, k0, v0, convw, convb, dtb, alog, dskip_x, normw,
                     sink_t, expand, nbatch, seq_len):
    nc = seq_len // CHUNK
    nproj = proj.shape[1]
    inner = normw.shape[1]
    conv_dim = convw.shape[1]
    gn = (conv_dim - inner) // 2
    kvd = k0.shape[2]
    n_heads = inner // SSM_HEAD_DIM
    cps = MIXER_CHUNKS_PER_STEP if nc % MIXER_CHUNKS_PER_STEP == 0 else 1
    nsteps = nc // cps
    row = lambda b, s: (b * nsteps + s, 0)
    ins, state_outs = _mixer_specs(cps * CHUNK, nproj, conv_dim, n_heads, kvd, inner, row,
                                   lambda b, s: (b, 0, 0), lambda b, s: (b, 0, 0, 0),
                                   lambda b, s: (0, 0), lambda b, s: (0, 0, 0))
    return pl.pallas_call(
        functools.partial(_even_mixer_kernel, first, (inner, gn, kvd), cps),
        grid=(nbatch, nsteps),
        in_specs=ins,
        out_specs=[pl.BlockSpec((cps * CHUNK, 2 * inner), row)] + state_outs,
        out_shape=[jax.ShapeDtypeStruct((nbatch * seq_len, 2 * inner), BF16)]
        + _mixer_state_shapes(nbatch, conv_dim, n_heads, kvd),
        scratch_shapes=_mixer_scratch(conv_dim, inner, gn),
        compiler_params=_cparams(("arbitrary", "arbitrary")),
        name="even_mixer",
    )(proj, pdt, conv0, ssm0, k0, v0, convw, convb, dtb, alog, dskip_x, normw, sink_t, expand)


def _pad_lanes(v):
    return jnp.pad(v, ((0, 0), (0, LANES - v.shape[1])))


def kernel(x_prompt, x_sample, c_prompt, c_sample, cache_k, cache_v, state_conv_a, state_ssm, state_conv_c,
           w_ada, b_ada, norm_pre, norm_post, w_in_even, conv_a_w, conv_a_b, dt_bias, a_log, d_skip, norm_ssm,
           sinks, w_out_even, w_in_odd, conv_c_w, w_out_odd):
    bp, lp, d = x_prompt.shape
    bs, ls, _ = x_sample.shape
    depth = w_ada.shape[0]
    inner = norm_ssm.shape[1]
    n_heads = dt_bias.shape[1]
    conv_dim = conv_a_w.shape[2]
    kvd = cache_k.shape[3] * cache_k.shape[4]
    width = conv_c_w.shape[2]
    assert ls == CHUNK and lp % CHUNK == 0 and inner == d and n_heads <= LANES

    mod = _mod_call(jnp.concatenate([c_prompt, c_sample], axis=0), w_ada, b_ada)

    xp = x_prompt.reshape(bp * lp, d)
    xs = x_sample.reshape(bs * ls, d)
    tm_p = min(TM_IN, lp)
    tm_s = bs * ls
    tm_out_p = min(TM_OUT, lp)
    expand = (jnp.arange(LANES)[:, None] == (jnp.arange(inner) // SSM_HEAD_DIM)[None, :]).astype(BF16)
    zeros_p = dict(
        conv_a=jnp.zeros((bp, SSM_CONV - 1, conv_dim), F32),
        ssm=jnp.zeros((bp, n_heads, SSM_HEAD_DIM, SSM_STATE), F32),
        kv=jnp.zeros((bp, WINDOW, kvd), F32),
        conv_c=jnp.zeros((bp, SHORT_CONV - 1, width), F32),
    )
    outs_p = dict(k=[], v=[], ca=[], ssm=[], cc=[])
    outs_s = dict(k=[], v=[], ca=[], ssm=[], cc=[])
    dt_lo = inner + conv_dim
    dt_hi = dt_lo + n_heads
    tc = TC_ODD
    w_main_all, w_dt_all = _cast_even_call(w_in_even, dt_lo, dt_hi, 128)
    w_in_odd_b = _cast_call(w_in_odd, 128)
    w_out_even_b = _cast_call(w_out_even, 512)
    w_out_odd_b = _cast_call(w_out_odd, 512)

    for layer in range(depth):
        i = layer // 2
        shift, scale, gate = jnp.split(mod[layer], 3, axis=-1)
        rep = lambda v: jnp.repeat(v[:bp], lp // CHUNK, axis=0)
        g_pre = norm_pre[layer][None, :]
        g_post = norm_post[layer][None, :]
        if layer % 2 == 0:
            mixer_params = (
                conv_a_w[i], conv_a_b[i][None, :], _pad_lanes(dt_bias[i][None, :]), _pad_lanes(a_log[i][None, :]),
                jnp.repeat(d_skip[i], SSM_HEAD_DIM)[None, :], norm_ssm[i][None, :],
                jnp.repeat(sinks[i], ATT_HEAD_DIM).reshape(ATT_KV_HEADS, 2, LANES), expand)
            proj, pdt = _inproj_even_call(xp, rep(scale), rep(shift), g_pre, w_main_all, w_dt_all, i, tm_p, TN_IN)
            y, ca, ssm, kn, vn = _even_mixer_call(True, proj, pdt, zeros_p["conv_a"], zeros_p["ssm"],
                                                  zeros_p["kv"], zeros_p["kv"], *mixer_params, bp, lp)
            xp = _outproj_call(y, w_out_even_b, i, xp, rep(gate), g_post, tm_out_p)
            outs_p["k"].append(kn); outs_p["v"].append(vn); outs_p["ca"].append(ca); outs_p["ssm"].append(ssm)
            proj, pdt = _inproj_even_call(xs, scale[bp:], shift[bp:], g_pre, w_main_all, w_dt_all, i, tm_s, TN_IN)
            y, ca, ssm, kn, vn = _even_mixer_call(False, proj, pdt, state_conv_a[i], state_ssm[i],
                                                  cache_k[i].reshape(bs, WINDOW, kvd),
                                                  cache_v[i].reshape(bs, WINDOW, kvd), *mixer_params, bs, ls)
            xs = _outproj_call(y, w_out_even_b, i, xs, gate[bp:], g_post, tm_s)
            outs_s["k"].append(kn); outs_s["v"].append(vn); outs_s["ca"].append(ca); outs_s["ssm"].append(ssm)
        else:
            y, cc = _inproj_odd_call(xp, rep(scale), rep(shift), g_pre, w_in_odd_b, i, conv_c_w[i],
                                     zeros_p["conv_c"], lp, tm_p, tc)
            xp = _outproj_call(y, w_out_odd_b, i, xp, rep(gate), g_post, tm_out_p)
            outs_p["cc"].append(cc)
            y, cc = _inproj_odd_call(xs, scale[bp:], shift[bp:], g_pre, w_in_odd_b, i, conv_c_w[i],
                                     state_conv_c[i], ls, tm_s, tc)
            xs = _outproj_call(y, w_out_odd_b, i, xs, gate[bp:], g_post, tm_s)
            outs_s["cc"].append(cc)

    kv_shape_p = (len(outs_p["k"]), bp, WINDOW) + cache_k.shape[3:]
    kv_shape_s = (len(outs_s["k"]), bs, WINDOW) + cache_k.shape[3:]
    return (xp.reshape(bp, lp, d), xs.reshape(bs, ls, d),
            jnp.stack(outs_p["k"]).reshape(kv_shape_p), jnp.stack(outs_p["v"]).reshape(kv_shape_p),
            jnp.stack(outs_p["ca"]), jnp.stack(outs_p["ssm"]), jnp.stack(outs_p["cc"]),
            jnp.stack(outs_s["k"]).reshape(kv_shape_s), jnp.stack(outs_s["v"]).reshape(kv_shape_s),
            jnp.stack(outs_s["ca"]), jnp.stack(outs_s["ssm"]), jnp.stack(outs_s["cc"]))
```

```python
import functools

import jax
import jax.numpy as jnp
from jax import lax
from jax.experimental import pallas as pl
from jax.experimental.pallas import tpu as pltpu

F32 = jnp.float32
BF16 = jnp.bfloat16

CHUNK = 64
WINDOW = 128
EPS = 1e-6
SSM_HEAD_DIM = 64
SSM_GROUPS = 4
SSM_STATE = 128
SSM_CONV = 4
ATT_HEAD_DIM = 64
ATT_KV_HEADS = 8
SHORT_CONV = 3
LANES = 128

VMEM_LIMIT = 56 * 1024 * 1024
VMEM_LIMIT_EVEN_IN = 60 * 1024 * 1024
TM_IN = 1024
TN_IN = 2048
TM_OUT = 512
TC_ODD = 512
MIXER_CHUNKS_PER_STEP = 4
ATT_HEADS_AFTER_CONV_PIECE = {}
ATT_HEADS_AFTER_GROUP = ((0, 1), (2, 3), (4, 5), (6, 7))
SUB_ROWS = 256


def _cparams(sem, vmem=VMEM_LIMIT):
    return pltpu.CompilerParams(dimension_semantics=sem, vmem_limit_bytes=vmem)


def _dot(a, b):
    return jnp.dot(a, b, preferred_element_type=F32)


def _dot_nt(a, b):
    return lax.dot_general(a, b, (((1,), (1,)), ((), ())), preferred_element_type=F32)


def _dot_tn(a, b):
    return lax.dot_general(a, b, (((0,), (0,)), ((), ())), preferred_element_type=F32)


def _silu(x):
    return x * jax.nn.sigmoid(x)


def _cast_kernel(x_ref, o_ref):
    o_ref[...] = x_ref[...].astype(BF16)


def _cast_call(w, tr):
    n, r, c = w.shape
    return pl.pallas_call(
        _cast_kernel,
        grid=(n, r // tr),
        in_specs=[pl.BlockSpec((None, tr, c), lambda l, i: (l, i, 0))],
        out_specs=pl.BlockSpec((None, tr, c), lambda l, i: (l, i, 0)),
        out_shape=jax.ShapeDtypeStruct(w.shape, BF16),
        compiler_params=_cparams(("arbitrary", "arbitrary")),
        name="cast_bf16",
    )(w)


def _cast_even_kernel(dt_lo, dt_hi, x_ref, o_ref, odt_ref):
    n_main = o_ref.shape[1]
    o_ref[:, 0:dt_lo] = x_ref[:, 0:dt_lo].astype(BF16)
    o_ref[:, dt_lo:n_main] = x_ref[:, dt_hi:dt_hi + n_main - dt_lo].astype(BF16)
    odt_ref[...] = jnp.zeros(odt_ref.shape, BF16)
    odt_ref[:, 0:dt_hi - dt_lo] = x_ref[:, dt_lo:dt_hi].astype(BF16)


def _cast_even_call(w, dt_lo, dt_hi, tr):
    n, r, c = w.shape
    n_main = c - (dt_hi - dt_lo)
    return pl.pallas_call(
        functools.partial(_cast_even_kernel, dt_lo, dt_hi),
        grid=(n, r // tr),
        in_specs=[pl.BlockSpec((None, tr, c), lambda l, i: (l, i, 0))],
        out_specs=[pl.BlockSpec((None, tr, n_main), lambda l, i: (l, i, 0)),
                   pl.BlockSpec((None, tr, LANES), lambda l, i: (l, i, 0))],
        out_shape=[jax.ShapeDtypeStruct((n, r, n_main), BF16),
                   jax.ShapeDtypeStruct((n, r, LANES), BF16)],
        compiler_params=_cparams(("arbitrary", "arbitrary")),
        name="cast_even",
    )(w)


def _mod_kernel(c_ref, w_ref, b_ref, o_ref):
    sc = _silu(c_ref[...]).astype(BF16)
    o_ref[0] = _dot(sc, w_ref[0].astype(BF16)) + b_ref[0]


def _mod_call(c_all, w_ada, b_ada):
    depth, d, d3 = w_ada.shape
    nb = c_all.shape[0]
    tn = 1024
    return pl.pallas_call(
        _mod_kernel,
        grid=(depth, d3 // tn),
        in_specs=[
            pl.BlockSpec((nb, d), lambda l, j: (0, 0)),
            pl.BlockSpec((1, d, tn), lambda l, j: (l, 0, j)),
            pl.BlockSpec((1, 1, tn), lambda l, j: (l, 0, j)),
        ],
        out_specs=pl.BlockSpec((1, nb, tn), lambda l, j: (l, 0, j)),
        out_shape=jax.ShapeDtypeStruct((depth, nb, d3), F32),
        compiler_params=_cparams(("arbitrary", "arbitrary")),
        name="adaln_mod",
    )(c_all, w_ada, b_ada.reshape(depth, 1, d3))


def _modulated_norm_into(x_ref, scale_ref, shift_ref, g_ref, h_scr, row0, row1):
    g = g_ref[...]
    for cc in range(row0 // CHUNK, row1 // CHUNK):
        rows = slice(cc * CHUNK, (cc + 1) * CHUNK)
        x = x_ref[rows, :]
        ms = jnp.mean(x * x, axis=-1, keepdims=True)
        y = x * lax.rsqrt(ms + EPS) * g
        h = y * (1.0 + scale_ref[cc:cc + 1, :]) + shift_ref[cc:cc + 1, :]
        h_scr[rows, :] = h.astype(BF16)


def _inproj_even_kernel(tm, sub, x_ref, scale_ref, shift_ref, g_ref, w_ref, wdt_ref, o_ref, odt_ref, h_scr):
    j = pl.program_id(1)

    @pl.when(j == 0)
    def _():
        norm = functools.partial(_modulated_norm_into, x_ref, scale_ref, shift_ref, g_ref, h_scr)
        norm(0, sub)
        for sb in range(tm // sub):
            rows = slice(sb * sub, (sb + 1) * sub)
            o_ref[rows, :] = _dot(h_scr[rows, :], w_ref[...])
            odt_ref[rows, :] = _dot(h_scr[rows, :], wdt_ref[...])
            if sb + 1 < tm // sub:
                norm((sb + 1) * sub, (sb + 2) * sub)

    @pl.when(j != 0)
    def _():
        o_ref[...] = _dot(h_scr[...], w_ref[...])


def _inproj_even_call(x, scale_pc, shift_pc, g, w_main, w_dt, li, tm, tn):
    r, d = x.shape
    n = w_main.shape[2]
    nch = tm // CHUNK
    return pl.pallas_call(
        functools.partial(_inproj_even_kernel, tm, min(SUB_ROWS, tm)),
        grid=(r // tm, n // tn),
        in_specs=[
            pl.BlockSpec((tm, d), lambda i, j: (i, 0)),
            pl.BlockSpec((nch, d), lambda i, j: (i, 0)),
            pl.BlockSpec((nch, d), lambda i, j: (i, 0)),
            pl.BlockSpec((1, d), lambda i, j: (0, 0)),
            pl.BlockSpec((None, d, tn), lambda i, j: (li, 0, j)),
            pl.BlockSpec((None, d, LANES), lambda i, j: (li, 0, 0)),
        ],
        out_specs=[
            pl.BlockSpec((tm, tn), lambda i, j: (i, j)),
            pl.BlockSpec((tm, LANES), lambda i, j: (i, 0)),
        ],
        out_shape=[
            jax.ShapeDtypeStruct((r, n), F32),
            jax.ShapeDtypeStruct((r, LANES), F32),
        ],
        scratch_shapes=[pltpu.VMEM((tm, d), BF16)],
        compiler_params=_cparams(("arbitrary", "arbitrary"), VMEM_LIMIT_EVEN_IN),
        name="inproj_even",
    )(x, scale_pc, shift_pc, g, w_main, w_dt)


def _inproj_odd_kernel(tm, sub, tc, tiles_per_seq, per_chunk_start,
                       x_ref, scale_ref, shift_ref, g_ref, wu_ref, wb_ref, wc_ref, wg_ref, cw_ref, c0_ref,
                       y_ref, cn_ref, h_scr, ebuf, halo):
    i = pl.program_id(0)
    j = pl.program_id(1)

    norm = functools.partial(_modulated_norm_into, x_ref, scale_ref, shift_ref, g_ref, h_scr)
    w0 = cw_ref[0:1, :]
    w1 = cw_ref[1:2, :]
    w2 = cw_ref[2:3, :]
    if not per_chunk_start:
        @pl.when(i % tiles_per_seq == 0)
        def _():
            ebuf[CHUNK + 6:CHUNK + 8, :] = c0_ref[0]

        @pl.when(i % tiles_per_seq != 0)
        def _():
            ebuf[CHUNK + 6:CHUNK + 8, :] = halo[j, 6:8, :]

    def body(with_norm):
        if with_norm:
            norm(0, sub)
        for sb in range(tm // sub):
            hs = h_scr[sb * sub:(sb + 1) * sub, :]
            pu = _dot(hs, wu_ref[...])
            pb = _dot(hs, wb_ref[...])
            pc = _dot(hs, wc_ref[...])
            pg = _dot(hs, wg_ref[...])
            if with_norm and sb + 1 < tm // sub:
                norm((sb + 1) * sub, (sb + 2) * sub)
            for c2 in range(sub // CHUNK):
                cc = sb * (sub // CHUNK) + c2
                r2 = slice(c2 * CHUNK, (c2 + 1) * CHUNK)
                rows = slice(cc * CHUNK, (cc + 1) * CHUNK)
                if per_chunk_start:
                    ebuf[6:8, :] = c0_ref[cc]
                else:
                    ebuf[6:8, :] = ebuf[CHUNK + 6:CHUNK + 8, :]
                ebuf[8:8 + CHUNK, :] = pc[r2] * pu[r2]
                conv = w0 * ebuf[6:6 + CHUNK, :] + w1 * ebuf[7:7 + CHUNK, :] + w2 * ebuf[8:8 + CHUNK, :]
                y_ref[rows, :] = (pb[r2] * conv * _silu(pg[r2])).astype(BF16)
                if per_chunk_start:
                    cn_ref[cc] = ebuf[CHUNK + 6:CHUNK + 8, :]

    pl.when(j == 0)(lambda: body(True))
    pl.when(j != 0)(lambda: body(False))
    if not per_chunk_start:
        tail = ebuf[CHUNK + 6:CHUNK + 8, :]
        halo[j, 6:8, :] = tail
        cn_ref[0] = tail


def _inproj_odd_call(x, scale_pc, shift_pc, g, w_all, li, conv_w, conv0, seq_len, tm, tc):
    r, d = x.shape
    width = conv_w.shape[1]
    nct = width // tc
    nch = tm // CHUNK
    sub = min(SUB_ROWS, tm)
    per_chunk_start = seq_len == CHUNK
    if per_chunk_start:
        tiles_per_seq = 1
        nb = nch
        c0_map = lambda i, j: (i, 0, j)
    else:
        assert seq_len % tm == 0
        tiles_per_seq = seq_len // tm
        nb = 1
        c0_map = lambda i, j: (i // tiles_per_seq, 0, j)
    wspec = lambda q: pl.BlockSpec((None, d, tc), lambda i, j: (li, 0, q * nct + j))
    y, tails = pl.pallas_call(
        functools.partial(_inproj_odd_kernel, tm, sub, tc, tiles_per_seq, per_chunk_start),
        grid=(r // tm, nct),
        in_specs=[
            pl.BlockSpec((tm, d), lambda i, j: (i, 0)),
            pl.BlockSpec((nch, d), lambda i, j: (i, 0)),
            pl.BlockSpec((nch, d), lambda i, j: (i, 0)),
            pl.BlockSpec((1, d), lambda i, j: (0, 0)),
            wspec(0), wspec(1), wspec(2), wspec(3),
            pl.BlockSpec((SHORT_CONV, tc), lambda i, j: (0, j)),
            pl.BlockSpec((nb, SHORT_CONV - 1, tc), c0_map),
        ],
        out_specs=[
            pl.BlockSpec((tm, tc), lambda i, j: (i, j)),
            pl.BlockSpec((nb, SHORT_CONV - 1, tc), lambda i, j: (i, 0, j)),
        ],
        out_shape=[
            jax.ShapeDtypeStruct((r, width), BF16),
            jax.ShapeDtypeStruct((r // tm * nb, SHORT_CONV - 1, width), F32),
        ],
        scratch_shapes=[
            pltpu.VMEM((tm, d), BF16),
            pltpu.VMEM((CHUNK + 8, tc), F32),
            pltpu.VMEM((nct, 8, tc), F32),
        ],
        compiler_params=_cparams(("arbitrary", "arbitrary")),
        name="inproj_odd",
    )(x, scale_pc, shift_pc, g, w_all, w_all, w_all, w_all, conv_w, conv0)
    return y, tails[tiles_per_seq - 1::tiles_per_seq]


def _outproj_kernel(tm, sub, y_ref, w_ref, x_ref, gate_ref, np_ref, o_ref):
    npw = np_ref[...]
    for sb in range(tm // sub):
        o = _dot(y_ref[sb * sub:(sb + 1) * sub, :], w_ref[...])
        for c2 in range(sub // CHUNK):
            cc = sb * (sub // CHUNK) + c2
            rows = slice(cc * CHUNK, (cc + 1) * CHUNK)
            oc = o[c2 * CHUNK:(c2 + 1) * CHUNK, :]
            ms = jnp.mean(oc * oc, axis=-1, keepdims=True)
            r = oc * lax.rsqrt(ms + EPS) * npw
            o_ref[rows, :] = x_ref[rows, :] + gate_ref[cc:cc + 1, :] * r


def _outproj_call(y, w_all, li, x, gate_pc, npost, tm):
    r, k = y.shape
    d = x.shape[1]
    nch = tm // CHUNK
    sub = min(SUB_ROWS, tm)
    return pl.pallas_call(
        functools.partial(_outproj_kernel, tm, sub),
        grid=(r // tm,),
        in_specs=[
            pl.BlockSpec((tm, k), lambda i: (i, 0)),
            pl.BlockSpec((None, k, d), lambda i: (li, 0, 0), pipeline_mode=pl.Buffered(1)),
            pl.BlockSpec((tm, d), lambda i: (i, 0)),
            pl.BlockSpec((nch, d), lambda i: (i, 0)),
            pl.BlockSpec((1, d), lambda i: (0, 0)),
        ],
        out_specs=pl.BlockSpec((tm, d), lambda i: (i, 0)),
        out_shape=jax.ShapeDtypeStruct((r, d), F32),
        compiler_params=_cparams(("arbitrary",)),
        name="outproj",
    )(y, w_all, x, gate_pc, npost)


def _write_kv_slot(kv, buf, base):
    lane = lax.broadcasted_iota(jnp.int32, (CHUNK, LANES), 1)
    lo = lane < ATT_HEAD_DIM
    for m in range(kv.shape[1] // LANES):
        blk = kv[:, m * LANES:(m + 1) * LANES]
        sw = pltpu.roll(blk, ATT_HEAD_DIM, 1)
        pieces = (
            (2 * m, jnp.where(lo, blk, 0.0), jnp.where(lo, 0.0, sw)),
            (2 * m + 1, jnp.where(lo, sw, 0.0), jnp.where(lo, 0.0, blk)),
        )
        for head, low, high in pieces:
            buf[head, pl.ds(base, CHUNK), 0:LANES] = low.astype(BF16)
            buf[head, pl.ds(base + CHUNK, CHUNK), 0:LANES] = high.astype(BF16)


def _mixer_chunk(first, dims, c, nc, maybe_first, maybe_last, write_y,
                 proj_ref, pdt_ref, conv0_ref, ssm0_ref, k0_ref, v0_ref,
                 convw_ref, convb_ref, dtb_ref, alog_ref, dskip_ref, normw_ref, sink_ref, e_ref,
                 convn_ref, ssmn_ref, kn_ref, vn_ref,
                 xext, xs_s, bc_s, ht_s, kbd, vbd):
    inner, gn, kvd = dims
    gw = inner // SSM_GROUPS
    off_xbc = inner
    off_q = off_xbc + inner + 2 * gn
    off_k = off_q + inner
    off_v = off_k + kvd
    off_g = off_v + kvd
    n_tr = inner // LANES

    lane64 = lax.broadcasted_iota(jnp.int32, (CHUNK, LANES), 1) < ATT_HEAD_DIM
    lane128 = lax.broadcasted_iota(jnp.int32, (2 * CHUNK, LANES), 1) < ATT_HEAD_DIM

    def _init():
        xext[5:8, :] = conv0_ref[0]
        for t in range(n_tr):
            ht_s[:, t * LANES:(t + 1) * LANES] = ssm0_ref[0, 2 * t:2 * t + 2].reshape(LANES, SSM_STATE).T
        for s in range(WINDOW // CHUNK):
            _write_kv_slot(k0_ref[0, s * CHUNK:(s + 1) * CHUNK, :], kbd, s * LANES)
            _write_kv_slot(v0_ref[0, s * CHUNK:(s + 1) * CHUNK, :], vbd, s * LANES)
        rowi = lax.broadcasted_iota(jnp.int32, (3 * LANES, LANES), 0)
        lanei = lax.broadcasted_iota(jnp.int32, (3 * LANES, LANES), 1)
        ones_bd = jnp.where((rowi % LANES < CHUNK) == (lanei < ATT_HEAD_DIM), 1.0, 0.0).astype(BF16)
        for h in range(ATT_KV_HEADS):
            vbd[h, :, LANES:2 * LANES] = ones_bd

    if maybe_first:
        pl.when(c == 0)(_init)

    def pj(cols):
        return proj_ref[:, cols]

    k_new = pj(slice(off_k, off_v))
    v_new = pj(slice(off_v, off_g))
    base = pl.multiple_of(lax.rem(c + 2, 3) * LANES, LANES)
    _write_kv_slot(k_new, kbd, base)
    _write_kv_slot(v_new, vbd, base)

    neg_inf = jnp.float32(-jnp.inf)
    qscale = ATT_HEAD_DIM ** -0.5

    def attn_head(kh):
        qa = slice(off_q + 2 * kh * LANES, off_q + (2 * kh + 1) * LANES)
        qb = slice(off_q + (2 * kh + 1) * LANES, off_q + (2 * kh + 2) * LANES)
        qs = (jnp.concatenate([pj(qa), pj(qb)], axis=0) * qscale).astype(BF16)
        s = _dot_nt(qs, kbd[kh])
        sb = [s[:, t * LANES:(t + 1) * LANES] for t in range(3)]
        if first:
            sb[0] = jnp.where(c == 0, neg_inf, sb[0])
            sb[1] = jnp.where(c <= 1, neg_inf, sb[1])
        mb = jnp.maximum(jnp.maximum(sb[0], sb[1]), sb[2])
        m_lo = jnp.max(jnp.where(lane128, mb, neg_inf), axis=-1, keepdims=True)
        m_hi = jnp.max(jnp.where(lane128, neg_inf, mb), axis=-1, keepdims=True)
        sink = jnp.concatenate([jnp.broadcast_to(sink_ref[kh, 0:1, :], (CHUNK, LANES)),
                                jnp.broadcast_to(sink_ref[kh, 1:2, :], (CHUNK, LANES))], axis=0)
        mx = jnp.maximum(jnp.where(lane128, m_lo, m_hi), sink)
        p = jnp.concatenate([jnp.exp(x - mx) for x in sb], axis=1).astype(BF16)
        ov = _dot(p, vbd[kh])
        denom = ov[:, LANES:2 * LANES] + jnp.exp(sink - mx)
        o = ov[:, 0:LANES] / denom
        ga = slice(off_g + 2 * kh * LANES, off_g + (2 * kh + 1) * LANES)
        gb = slice(off_g + (2 * kh + 1) * LANES, off_g + (2 * kh + 2) * LANES)
        ya = slice(inner + 2 * kh * LANES, inner + (2 * kh + 1) * LANES)
        yb = slice(inner + (2 * kh + 1) * LANES, inner + (2 * kh + 2) * LANES)
        write_y(ya, (o[0:CHUNK] * _silu(pj(ga))).astype(BF16))
        write_y(yb, (o[CHUNK:2 * CHUNK] * _silu(pj(gb))).astype(BF16))


    xext[8:8 + CHUNK, :] = pj(slice(off_xbc, off_q))
    piece = 512
    for t in range((inner + 2 * gn) // piece):
        for kh in ATT_HEADS_AFTER_CONV_PIECE.get(t, ()):
            attn_head(kh)
        cs = slice(t * piece, (t + 1) * piece)
        acc = convb_ref[:, cs] + convw_ref[3:4, cs] * xext[8:8 + CHUNK, cs]
        for i in range(SSM_CONV - 1):
            acc = acc + convw_ref[i:i + 1, cs] * xext[5 + i:5 + i + CHUNK, cs]
        act = _silu(acc)
        if t * piece < inner:
            xs_s[:, cs] = act
        else:
            bc_s[:, t * piece - inner:(t + 1) * piece - inner] = act
    tail = xext[CHUNK + 5:CHUNK + 8, :]
    convn_ref[0] = tail
    xext[5:8, :] = tail

    dt = jax.nn.softplus(pdt_ref[...] + dtb_ref[...])
    da = dt * (-jnp.exp(alog_ref[...]))
    ri = lax.broadcasted_iota(jnp.int32, (CHUNK, CHUNK), 0)
    ci = lax.broadcasted_iota(jnp.int32, (CHUNK, CHUNK), 1)
    causal = ri >= ci
    tri = jnp.where(causal, 1.0, 0.0).astype(BF16)
    d_hi = da.astype(BF16)
    r1 = da - d_hi.astype(F32)
    d_mid = r1.astype(BF16)
    d_lo = (r1 - d_mid.astype(F32)).astype(BF16)
    a_cum = _dot(tri, d_hi) + _dot(tri, d_mid) + _dot(tri, d_lo)
    a_last = a_cum[CHUNK - 1:CHUNK, :]
    stack = jnp.concatenate([jnp.exp(a_cum), dt, jnp.exp(a_last - a_cum)], axis=0)
    s_hi = stack.astype(BF16)
    s_lo = (stack - s_hi.astype(F32)).astype(BF16)
    act_t = jnp.concatenate([a_cum, jnp.zeros((CHUNK, LANES), F32)], axis=0).T

    pairs_per_group = gw // LANES

    def ssd_group(g):
        gcs = slice(g * gw, (g + 1) * gw)
        ex = _dot(s_hi, e_ref[:, gcs]) + _dot(s_lo, e_ref[:, gcs])
        ea = ex[0:CHUNK]
        xs = xs_s[:, gcs]
        xdt = xs * ex[CHUNK:2 * CHUNK]
        xdt_b = xdt.astype(BF16)
        xsc_b = (xdt * ex[2 * CHUNK:3 * CHUNK]).astype(BF16)
        bg = bc_s[:, g * SSM_STATE:(g + 1) * SSM_STATE].astype(BF16)
        cg = bc_s[:, gn + g * SSM_STATE:gn + (g + 1) * SSM_STATE].astype(BF16)
        cb = _dot_nt(cg, bg)
        h_prev = ht_s[:, gcs]
        y_off = _dot(cg, h_prev.astype(BF16)) * ea
        ht_s[:, gcs] = h_prev * ea[CHUNK - 1:CHUNK, :] + _dot_tn(bg, xsc_b)
        y_pairs = []
        for pm in range(pairs_per_group):
            m = g * pairs_per_group + pm
            ps = slice(m * LANES, (m + 1) * LANES)
            pl_ = slice(pm * LANES, (pm + 1) * LANES)
            halves = []
            for e in range(2):
                h = 2 * m + e
                seg = a_cum[:, h:h + 1] - act_t[h:h + 1, 0:CHUNK]
                dec = jnp.exp(jnp.where(causal, seg, -jnp.inf))
                halves.append(_dot((cb * dec).astype(BF16), xdt_b[:, pl_]))
            yd = jnp.where(lane64, halves[0], halves[1])
            y_pairs.append(yd + y_off[:, pl_] + dskip_ref[:, ps] * xs[:, pl_])
        yg = jnp.concatenate(y_pairs, axis=1) * _silu(pj(gcs))
        yg = yg * lax.rsqrt(jnp.mean(yg * yg, axis=-1, keepdims=True) + EPS)
        write_y(gcs, (yg * normw_ref[:, gcs]).astype(BF16))

    for g in range(SSM_GROUPS):
        ssd_group(g)
        for kh in ATT_HEADS_AFTER_GROUP[g]:
            attn_head(kh)

    def _window_from_state():
        kn_ref[0, 0:CHUNK, :] = k0_ref[0, CHUNK:2 * CHUNK, :]
        vn_ref[0, 0:CHUNK, :] = v0_ref[0, CHUNK:2 * CHUNK, :]

    def _window_shift():
        kn_ref[0, 0:CHUNK, :] = kn_ref[0, CHUNK:2 * CHUNK, :]
        vn_ref[0, 0:CHUNK, :] = vn_ref[0, CHUNK:2 * CHUNK, :]

    if maybe_first:
        pl.when(c == 0)(_window_from_state)
        pl.when(c != 0)(_window_shift)
    else:
        _window_shift()
    kn_ref[0, CHUNK:2 * CHUNK, :] = pj(slice(off_k, off_v))
    vn_ref[0, CHUNK:2 * CHUNK, :] = pj(slice(off_v, off_g))

    def _final_state():
        for t in range(n_tr):
            ssmn_ref[0, 2 * t:2 * t + 2] = ht_s[:, t * LANES:(t + 1) * LANES].T.reshape(
                2, SSM_HEAD_DIM, SSM_STATE)

    if maybe_last:
        pl.when(c == nc - 1)(_final_state)


N_MIXER_INPUTS = 14


def _even_mixer_kernel(first, dims, cps, *refs):
    ins, y_ref, rest = refs[:N_MIXER_INPUTS], refs[N_MIXER_INPUTS], refs[N_MIXER_INPUTS + 1:]
    proj_ref, pdt_ref = ins[:2]
    for u in range(cps):
        rows = pl.ds(u * CHUNK, CHUNK)

        def write_y(cols, val, rows=rows):
            y_ref[rows, cols] = val

        _mixer_chunk(first, dims, pl.program_id(1) * cps + u, pl.num_programs(1) * cps,
                     u == 0, u == cps - 1, write_y,
                     proj_ref.at[rows], pdt_ref.at[rows], *ins[2:], *rest)


def _mixer_specs(rows, nproj, conv_dim, n_heads, kvd, inner, row, per_b3, per_b4, const2, const3):
    ins = [
        pl.BlockSpec((rows, nproj), row),
        pl.BlockSpec((rows, LANES), row),
        pl.BlockSpec((1, SSM_CONV - 1, conv_dim), per_b3),
        pl.BlockSpec((1, n_heads, SSM_HEAD_DIM, SSM_STATE), per_b4),
        pl.BlockSpec((1, WINDOW, kvd), per_b3),
        pl.BlockSpec((1, WINDOW, kvd), per_b3),
        pl.BlockSpec((SSM_CONV, conv_dim), const2),
        pl.BlockSpec((1, conv_dim), const2),
        pl.BlockSpec((1, LANES), const2),
        pl.BlockSpec((1, LANES), const2),
        pl.BlockSpec((1, inner), const2),
        pl.BlockSpec((1, inner), const2),
        pl.BlockSpec((ATT_KV_HEADS, 2, LANES), const3),
        pl.BlockSpec((LANES, inner), const2),
    ]
    assert len(ins) == N_MIXER_INPUTS
    state_outs = [
        pl.BlockSpec((1, SSM_CONV - 1, conv_dim), per_b3),
        pl.BlockSpec((1, n_heads, SSM_HEAD_DIM, SSM_STATE), per_b4),
        pl.BlockSpec((1, WINDOW, kvd), per_b3),
        pl.BlockSpec((1, WINDOW, kvd), per_b3),
    ]
    return ins, state_outs


def _mixer_state_shapes(nbatch, conv_dim, n_heads, kvd):
    return [
        jax.ShapeDtypeStruct((nbatch, SSM_CONV - 1, conv_dim), F32),
        jax.ShapeDtypeStruct((nbatch, n_heads, SSM_HEAD_DIM, SSM_STATE), F32),
        jax.ShapeDtypeStruct((nbatch, WINDOW, kvd), F32),
        jax.ShapeDtypeStruct((nbatch, WINDOW, kvd), F32),
    ]


def _mixer_scratch(conv_dim, inner, gn):
    return [
        pltpu.VMEM((CHUNK + 8, conv_dim), F32),
        pltpu.VMEM((CHUNK, inner), F32),
        pltpu.VMEM((CHUNK, 2 * gn), F32),
        pltpu.VMEM((SSM_STATE, inner), F32),
        pltpu.VMEM((ATT_KV_HEADS, 3 * LANES, LANES), BF16),
        pltpu.VMEM((ATT_KV_HEADS, 3 * LANES, 2 * LANES), BF16),
    ]


def _even_mixer_call(first, proj, pdt, conv0, ssm0, k0, v0, convw, convb, dtb, alog, dskip_x, normw,
                     sink_t, expand, nbatch, seq_len):
    nc = seq_len // CHUNK
    nproj = proj.shape[1]
    inner = normw.shape[1]
    conv_dim = convw.shape[1]
    gn = (conv_dim - inner) // 2
    kvd = k0.shape[2]
    n_heads = inner // SSM_HEAD_DIM
    cps = MIXER_CHUNKS_PER_STEP if nc % MIXER_CHUNKS_PER_STEP == 0 else 1
    nsteps = nc // cps
    row = lambda b, s: (b * nsteps + s, 0)
    ins, state_outs = _mixer_specs(cps * CHUNK, nproj, conv_dim, n_heads, kvd, inner, row,
                                   lambda b, s: (b, 0, 0), lambda b, s: (b, 0, 0, 0),
                                   lambda b, s: (0, 0), lambda b, s: (0, 0, 0))
    return pl.pallas_call(
        functools.partial(_even_mixer_kernel, first, (inner, gn, kvd), cps),
        grid=(nbatch, nsteps),
        in_specs=ins,
        out_specs=[pl.BlockSpec((cps * CHUNK, 2 * inner), row)] + state_outs,
        out_shape=[jax.ShapeDtypeStruct((nbatch * seq_len, 2 * inner), BF16)]
        + _mixer_state_shapes(nbatch, conv_dim, n_heads, kvd),
        scratch_shapes=_mixer_scratch(conv_dim, inner, gn),
        compiler_params=_cparams(("arbitrary", "arbitrary")),
        name="even_mixer",
    )(proj, pdt, conv0, ssm0, k0, v0, convw, convb, dtb, alog, dskip_x, normw, sink_t, expand)


def _pad_lanes(v):
    return jnp.pad(v, ((0, 0), (0, LANES - v.shape[1])))


def kernel(x_prompt, x_sample, c_prompt, c_sample, cache_k, cache_v, state_conv_a, state_ssm, state_conv_c,
           w_ada, b_ada, norm_pre, norm_post, w_in_even, conv_a_w, conv_a_b, dt_bias, a_log, d_skip, norm_ssm,
           sinks, w_out_even, w_in_odd, conv_c_w, w_out_odd):
    bp, lp, d = x_prompt.shape
    bs, ls, _ = x_sample.shape
    depth = w_ada.shape[0]
    inner = norm_ssm.shape[1]
    n_heads = dt_bias.shape[1]
    conv_dim = conv_a_w.shape[2]
    kvd = cache_k.shape[3] * cache_k.shape[4]
    width = conv_c_w.shape[2]
    assert ls == CHUNK and lp % CHUNK == 0 and inner == d and n_heads <= LANES

    mod = _mod_call(jnp.concatenate([c_prompt, c_sample], axis=0), w_ada, b_ada)

    xp = x_prompt.reshape(bp * lp, d)
    xs = x_sample.reshape(bs * ls, d)
    tm_p = min(TM_IN, lp)
    tm_s = bs * ls
    tm_out_p = min(TM_OUT, lp)
    expand = (jnp.arange(LANES)[:, None] == (jnp.arange(inner) // SSM_HEAD_DIM)[None, :]).astype(BF16)
    zeros_p = dict(
        conv_a=jnp.zeros((bp, SSM_CONV - 1, conv_dim), F32),
        ssm=jnp.zeros((bp, n_heads, SSM_HEAD_DIM, SSM_STATE), F32),
        kv=jnp.zeros((bp, WINDOW, kvd), F32),
        conv_c=jnp.zeros((bp, SHORT_CONV - 1, width), F32),
    )
    outs_p = dict(k=[], v=[], ca=[], ssm=[], cc=[])
    outs_s = dict(k=[], v=[], ca=[], ssm=[], cc=[])
    dt_lo = inner + conv_dim
    dt_hi = dt_lo + n_heads
    tc = TC_ODD
    w_main_all, w_dt_all = _cast_even_call(w_in_even, dt_lo, dt_hi, 128)
    w_in_odd_b = _cast_call(w_in_odd, 128)
    w_out_even_b = _cast_call(w_out_even, 512)
    w_out_odd_b = _cast_call(w_out_odd, 512)

    for layer in range(depth):
        i = layer // 2
        shift, scale, gate = jnp.split(mod[layer], 3, axis=-1)
        rep = lambda v: jnp.repeat(v[:bp], lp // CHUNK, axis=0)
        g_pre = norm_pre[layer][None, :]
        g_post = norm_post[layer][None, :]
        if layer % 2 == 0:
            mixer_params = (
                conv_a_w[i], conv_a_b[i][None, :], _pad_lanes(dt_bias[i][None, :]), _pad_lanes(a_log[i][None, :]),
                jnp.repeat(d_skip[i], SSM_HEAD_DIM)[None, :], norm_ssm[i][None, :],
                jnp.repeat(sinks[i], ATT_HEAD_DIM).reshape(ATT_KV_HEADS, 2, LANES), expand)
            proj, pdt = _inproj_even_call(xp, rep(scale), rep(shift), g_pre, w_main_all, w_dt_all, i, tm_p, TN_IN)
            y, ca, ssm, kn, vn = _even_mixer_call(True, proj, pdt, zeros_p["conv_a"], zeros_p["ssm"],
                                                  zeros_p["kv"], zeros_p["kv"], *mixer_params, bp, lp)
            xp = _outproj_call(y, w_out_even_b, i, xp, rep(gate), g_post, tm_out_p)
            outs_p["k"].append(kn); outs_p["v"].append(vn); outs_p["ca"].append(ca); outs_p["ssm"].append(ssm)
            proj, pdt = _inproj_even_call(xs, scale[bp:], shift[bp:], g_pre, w_main_all, w_dt_all, i, tm_s, TN_IN)
            y, ca, ssm, kn, vn = _even_mixer_call(False, proj, pdt, state_conv_a[i], state_ssm[i],
                                                  cache_k[i].reshape(bs, WINDOW, kvd),
                                                  cache_v[i].reshape(bs, WINDOW, kvd), *mixer_params, bs, ls)
            xs = _outproj_call(y, w_out_even_b, i, xs, gate[bp:], g_post, tm_s)
            outs_s["k"].append(kn); outs_s["v"].append(vn); outs_s["ca"].append(ca); outs_s["ssm"].append(ssm)
        else:
            y, cc = _inproj_odd_call(xp, rep(scale), rep(shift), g_pre, w_in_odd_b, i, conv_c_w[i],
                                     zeros_p["conv_c"], lp, tm_p, tc)
            xp = _outproj_call(y, w_out_odd_b, i, xp, rep(gate), g_post, tm_out_p)
            outs_p["cc"].append(cc)
            y, cc = _inproj_odd_call(xs, scale[bp:], shift[bp:], g_pre, w_in_odd_b, i, conv_c_w[i],
                                     state_conv_c[i], ls, tm_s, tc)
            xs = _outproj_call(y, w_out_odd_b, i, xs, gate[bp:], g_post, tm_s)
            outs_s["cc"].append(cc)

    kv_shape_p = (len(outs_p["k"]), bp, WINDOW) + cache_k.shape[3:]
    kv_shape_s = (len(outs_s["k"]), bs, WINDOW) + cache_k.shape[3:]
    return (xp.reshape(bp, lp, d), xs.reshape(bs, ls, d),
            jnp.stack(outs_p["k"]).reshape(kv_shape_p), jnp.stack(outs_p["v"]).reshape(kv_shape_p),
            jnp.stack(outs_p["ca"]), jnp.stack(outs_p["ssm"]), jnp.stack(outs_p["cc"]),
            jnp.stack(outs_s["k"]).reshape(kv_shape_s), jnp.stack(outs_s["v"]).reshape(kv_shape_s),
            jnp.stack(outs_s["ca"]), jnp.stack(outs_s["ssm"]), jnp.stack(outs_s["cc"]))
```
